```python
import jax, jax.numpy as jnp
from jax import lax
import numpy as np

D_MODEL = 1024
BATCH = 8
SEQ = 2048
DEPTH = 4

GRID_W = 64
CTX_LEN = 256
HEAD_DIM = 64
N_Q_HEADS = 8
N_KV_HEADS = 2
GQA_GROUP = N_Q_HEADS // N_KV_HEADS
ATTN_WIDTH = N_Q_HEADS * HEAD_DIM
KV_WIDTH = N_KV_HEADS * HEAD_DIM
CONV_WIDTH = D_MODEL - ATTN_WIDTH
CONV_KERNEL = 31
Q_BLOCK = 128
D_FF = -(-8 * D_MODEL // (3 * 256)) * 256
ROPE_THETA = 10000.0
DN_ALPHA = (2 * DEPTH) ** 0.25
DN_BETA = (8 * DEPTH) ** -0.25
LN_EPS = 1e-6
RMS_EPS = 1e-6
ATTN_SCALE = HEAD_DIM ** -0.5
IN_WIDTH = ATTN_WIDTH + 2 * KV_WIDTH + 2 * CONV_WIDTH
SPLITS = (ATTN_WIDTH, ATTN_WIDTH + KV_WIDTH, ATTN_WIDTH + 2 * KV_WIDTH,
          ATTN_WIDTH + 2 * KV_WIDTH + CONV_WIDTH)

kernel_name = "hymba_conformer_gqa_deepnorm_dit"


def layer_norm(x, g, b):
    xf = x.astype(jnp.float32)
    mu = xf.mean(-1, keepdims=True)
    var = jnp.square(xf - mu).mean(-1, keepdims=True)
    return ((xf - mu) * lax.rsqrt(var + LN_EPS) * g + b).astype(x.dtype)


def rms_norm(x, g):
    xf = x.astype(jnp.float32)
    return (xf * lax.rsqrt(jnp.square(xf).mean(-1, keepdims=True) + RMS_EPS) * g).astype(x.dtype)


def axial_rope_tables(rows):
    r, col = jnp.meshgrid(jnp.arange(rows), jnp.arange(GRID_W), indexing="ij")
    pos = jnp.stack([r.reshape(-1), col.reshape(-1)], axis=-1).astype(jnp.float32)
    n_freq = HEAD_DIM // 4
    freqs = ROPE_THETA ** (-jnp.arange(n_freq, dtype=jnp.float32) / n_freq)
    ang = pos[:, :, None] * freqs
    return jnp.cos(ang), jnp.sin(ang)


def apply_axial_rope(x, cos, sin):
    lead = x.shape[:-1]
    xa = x.astype(jnp.float32).reshape(*lead, 2, 2, HEAD_DIM // 4)
    x1, x2 = xa[..., 0, :], xa[..., 1, :]
    expand = (1, cos.shape[0]) + (1,) * (x.ndim - 3) + cos.shape[1:]
    cs, sn = cos.reshape(expand), sin.reshape(expand)
    out = jnp.stack([x1 * cs - x2 * sn, x2 * cs + x1 * sn], axis=-2)
    return out.reshape(x.shape).astype(x.dtype)


def project(h, w):
    B, L, _ = h.shape
    q, k, v, u, g = jnp.split(h @ w, SPLITS, axis=-1)
    q = q.reshape(B, L, N_KV_HEADS, GQA_GROUP, HEAD_DIM)
    k = k.reshape(B, L, N_KV_HEADS, HEAD_DIM)
    v = v.reshape(B, L, N_KV_HEADS, HEAD_DIM)
    return q, k, v, u, g


def project_kv(h, w):
    B, L, _ = h.shape
    k, v = jnp.split(h @ w[:, SPLITS[0]:SPLITS[2]], 2, axis=-1)
    return k.reshape(B, L, N_KV_HEADS, HEAD_DIM), v.reshape(B, L, N_KV_HEADS, HEAD_DIM)


def gqa_attend(q, k, v):
    s = jnp.einsum("bqhgd,bkhd->bhgqk", q, k, preferred_element_type=jnp.float32) * ATTN_SCALE
    p = jax.nn.softmax(s, axis=-1).astype(v.dtype)
    return jnp.einsum("bhgqk,bkhd->bqhgd", p, v)


def latent_attention(q, k_all, v_all):
    B, n = q.shape[:2]
    nblk = n // Q_BLOCK
    qb = q.reshape(B, nblk, Q_BLOCK, N_KV_HEADS, GQA_GROUP, HEAD_DIM).transpose(1, 0, 2, 3, 4, 5)
    ob = lax.map(lambda blk: gqa_attend(blk, k_all, v_all), qb)
    return ob.transpose(1, 0, 2, 3, 4, 5).reshape(B, n, ATTN_WIDTH)


def conformer_conv(u, g, dw_w, dw_b, ln_g, ln_b):
    a = u * jax.nn.sigmoid(g)
    y = lax.conv_general_dilated(
        a, dw_w.astype(a.dtype), window_strides=(1,),
        padding=[(CONV_KERNEL // 2, CONV_KERNEL // 2)],
        dimension_numbers=("NWC", "WIO", "NWC"), feature_group_count=a.shape[-1]) + dw_b
    return jax.nn.silu(layer_norm(y, ln_g, ln_b))


def swiglu(h, w1, w3, w2):
    return (jax.nn.silu(h @ w1) * (h @ w3)) @ w2


def setup_inputs(seed: int = 0) -> dict:
    key = jax.random.key(seed)
    ks = jax.random.split(key, 24)
    f32 = jnp.float32
    nrm = lambda k, shape, s: jax.random.normal(k, shape, f32) * s
    D = D_MODEL
    return {
        "x": nrm(ks[0], (BATCH, SEQ, D), 1.0),
        "c": nrm(ks[1], (BATCH, D), 1.0),
        "ctx": nrm(ks[2], (BATCH, CTX_LEN, D), 1.0),
        "c_ctx": nrm(ks[3], (D,), 1.0),
        "w_ada": nrm(ks[4], (DEPTH, D, 6 * D), 0.5 * D ** -0.5),
        "b_ada": nrm(ks[5], (DEPTH, 6 * D), 0.02),
        "w_in": nrm(ks[6], (DEPTH, D, IN_WIDTH), D ** -0.5),
        "q_norm_g": 1.0 + nrm(ks[7], (DEPTH, HEAD_DIM), 0.02),
        "k_norm_g": 1.0 + nrm(ks[8], (DEPTH, HEAD_DIM), 0.02),
        "dw_w": nrm(ks[9], (DEPTH, CONV_KERNEL, 1, CONV_WIDTH), CONV_KERNEL ** -0.5),
        "dw_b": nrm(ks[10], (DEPTH, CONV_WIDTH), 0.02),
        "conv_ln_g": 1.0 + nrm(ks[11], (DEPTH, CONV_WIDTH), 0.02),
        "conv_ln_b": nrm(ks[12], (DEPTH, CONV_WIDTH), 0.02),
        "w_out": nrm(ks[13], (DEPTH, D, D), DN_BETA * D ** -0.5),
        "ln1_g": 1.0 + nrm(ks[14], (DEPTH, D), 0.02),
        "ln1_b": nrm(ks[15], (DEPTH, D), 0.02),
        "w_ff1": nrm(ks[16], (DEPTH, D, D_FF), D ** -0.5),
        "w_ff3": nrm(ks[17], (DEPTH, D, D_FF), D ** -0.5),
        "w_ff2": nrm(ks[18], (DEPTH, D_FF, D), DN_BETA * D_FF ** -0.5),
        "ln2_g": 1.0 + nrm(ks[19], (DEPTH, D), 0.02),
        "ln2_b": nrm(ks[20], (DEPTH, D), 0.02),
    }


def reference(x, c, ctx, c_ctx, w_ada, b_ada, w_in, q_norm_g, k_norm_g, dw_w, dw_b,
              conv_ln_g, conv_ln_b, w_out, ln1_g, ln1_b, w_ff1, w_ff3, w_ff2, ln2_g, ln2_b):
    B, n, _ = x.shape
    Lc = ctx.shape[1]
    rows = n // GRID_W
    cos, sin = axial_rope_tables(rows)
    s_lat = jax.nn.silu(c)
    s_ctx = jax.nn.silu(c_ctx)
    for l in range(DEPTH):
        last = l == DEPTH - 1
        mod_l = (s_lat @ w_ada[l] + b_ada[l])[:, None, :]
        mod_c = (s_ctx @ w_ada[l] + b_ada[l])[None, None, :]
        sh1, sc1, g1, sh2, sc2, g2 = jnp.split(mod_l, 6, axis=-1)
        csh1, csc1, cg1, csh2, csc2, cg2 = jnp.split(mod_c, 6, axis=-1)

        hl = x * (1.0 + sc1) + sh1
        hc = ctx * (1.0 + csc1) + csh1
        ql, kl, vl, ul, gl = project(hl, w_in[l])
        ql = apply_axial_rope(rms_norm(ql, q_norm_g[l]), cos, sin)
        kl = apply_axial_rope(rms_norm(kl, k_norm_g[l]), cos, sin)
        if last:
            kc, vc = project_kv(hc, w_in[l])
        else:
            qc, kc, vc, uc, gc = project(hc, w_in[l])
        kc = rms_norm(kc, k_norm_g[l])
        k_all = jnp.concatenate([kc, kl], axis=1)
        v_all = jnp.concatenate([vc, vl], axis=1)
        attn_l = latent_attention(ql, k_all, v_all)
        conv_l = conformer_conv(ul, gl, dw_w[l], dw_b[l], conv_ln_g[l], conv_ln_b[l])
        y_l = jnp.concatenate([attn_l, conv_l], axis=-1) @ w_out[l]
        x_new = layer_norm(DN_ALPHA * x + g1 * y_l, ln1_g[l], ln1_b[l])

        hl2 = x_new * (1.0 + sc2) + sh2
        x = layer_norm(DN_ALPHA * x_new + g2 * swiglu(hl2, w_ff1[l], w_ff3[l], w_ff2[l]),
                       ln2_g[l], ln2_b[l])

        if not last:
            qc = rms_norm(qc, q_norm_g[l])
            attn_c = gqa_attend(qc, kc, vc).reshape(B, Lc, ATTN_WIDTH)
            conv_c = conformer_conv(uc, gc, dw_w[l], dw_b[l], conv_ln_g[l], conv_ln_b[l])
            y_c = jnp.concatenate([attn_c, conv_c], axis=-1) @ w_out[l]
            ctx_new = layer_norm(DN_ALPHA * ctx + cg1 * y_c, ln1_g[l], ln1_b[l])
            hc2 = ctx_new * (1.0 + csc2) + csh2
            ctx = layer_norm(DN_ALPHA * ctx_new + cg2 * swiglu(hc2, w_ff1[l], w_ff3[l], w_ff2[l]),
                             ln2_g[l], ln2_b[l])
    return x
```

```python
import functools

import jax
import jax.numpy as jnp
from jax import lax
from jax.experimental import pallas as pl
from jax.experimental.pallas import tpu as pltpu

F32 = jnp.float32
BF16 = jnp.bfloat16

D_MODEL = 1024
DEPTH = 4
GRID_W = 64
HEAD_DIM = 64
N_Q_HEADS = 8
N_KV_HEADS = 2
ATTN_WIDTH = N_Q_HEADS * HEAD_DIM
KV_WIDTH = N_KV_HEADS * HEAD_DIM
CONV_WIDTH = D_MODEL - ATTN_WIDTH
CONV_KERNEL = 31
CONV_HALF = CONV_KERNEL // 2
D_FF = 2816
ROPE_THETA = 10000.0
DN_ALPHA = (2 * DEPTH) ** 0.25
LN_EPS = 1e-6
RMS_EPS = 1e-6
ATTN_SCALE = HEAD_DIM ** -0.5
IN_WIDTH = ATTN_WIDTH + 2 * KV_WIDTH + 2 * CONV_WIDTH

LANES = 128
HALO = 16
CONV_ROWS = 32
MOD_ROWS = 16
VMEM_LIMIT = 52 * 1024 * 1024
FF_CHUNKS = ((0, 1536), (1536, 1280))

NT_DIMS = (((1,), (1,)), ((), ()))


def _const_spec(shape):
    zeros = (0,) * len(shape)
    return pl.BlockSpec(shape, lambda *_: zeros, pipeline_mode=pl.Buffered(1))


def _layer_norm(z, g, b):
    mu = jnp.mean(z, axis=-1, keepdims=True)
    zc = z - mu
    var = jnp.mean(zc * zc, axis=-1, keepdims=True)
    return zc * lax.rsqrt(var + LN_EPS) * g + b


def _ada_kernel(c_ref, w_ref, b_ref, o_ref):
    c = c_ref[...]
    s = (c * jax.nn.sigmoid(c)).astype(BF16)
    o_ref[...] = jnp.dot(s, w_ref[...].astype(BF16), preferred_element_type=F32) + b_ref[...]


def _ada_call(c_all, w_ada, b_ada):
    n_col = 6 * D_MODEL // D_MODEL
    return pl.pallas_call(
        _ada_kernel,
        grid=(DEPTH, n_col),
        in_specs=[
            pl.BlockSpec((MOD_ROWS, D_MODEL), lambda l, j: (0, 0)),
            pl.BlockSpec((None, D_MODEL, D_MODEL), lambda l, j: (l, 0, j)),
            pl.BlockSpec((None, 1, D_MODEL), lambda l, j: (l, 0, j)),
        ],
        out_specs=pl.BlockSpec((None, MOD_ROWS, D_MODEL), lambda l, j: (l, 0, j)),
        out_shape=jax.ShapeDtypeStruct((DEPTH, MOD_ROWS, 6 * D_MODEL), F32),
        compiler_params=pltpu.CompilerParams(
            dimension_semantics=("arbitrary", "arbitrary"), vmem_limit_bytes=VMEM_LIMIT),
        name="ada_mod",
    )(c_all, w_ada, b_ada)


def _head_norm_rope(z, gain, bd, cos, sin):
    ssq = jnp.dot((z * z).astype(BF16), bd, preferred_element_type=F32)
    zn = z * lax.rsqrt(ssq * (1.0 / HEAD_DIM) + RMS_EPS) * gain
    outs = []
    for c0 in range(0, z.shape[1], LANES):
        zc = zn[:, c0:c0 + LANES]
        up = pltpu.roll(zc, LANES - 16, 1)
        dn = pltpu.roll(zc, 16, 1)
        lane = lax.broadcasted_iota(jnp.int32, zc.shape, 1)
        partner = jnp.where((lane & 16) == 0, up, dn)
        outs.append(zc * cos + partner * sin)
    return outs


def _kv_variants(z):
    zs = pltpu.roll(z, HEAD_DIM, 1)
    lo = lax.broadcasted_iota(jnp.int32, z.shape, 1) < HEAD_DIM
    zero = jnp.zeros_like(z)
    return jnp.concatenate(
        [jnp.where(lo, z, zero), jnp.where(lo, zero, zs), jnp.where(lo, zs, zero), jnp.where(lo, zero, z)],
        axis=-1).astype(BF16)


def _inproj_kernel(x_ref, mod_ref, w_ref, gq_ref, gk_ref, cos_ref, sin_ref, bd_ref,
                   q_ref, kk_ref, vv_ref, a_ref):
    h = (x_ref[...] * (1.0 + mod_ref[1:2, :]) + mod_ref[0:1, :]).astype(BF16)
    proj = jnp.dot(h, w_ref[...], preferred_element_type=F32)
    cos, sin = cos_ref[...], sin_ref[...]
    q = proj[:, :ATTN_WIDTH]
    k = proj[:, ATTN_WIDTH:ATTN_WIDTH + KV_WIDTH]
    v = proj[:, ATTN_WIDTH + KV_WIDTH:ATTN_WIDTH + 2 * KV_WIDTH]
    u = proj[:, ATTN_WIDTH + 2 * KV_WIDTH:ATTN_WIDTH + 2 * KV_WIDTH + CONV_WIDTH]
    g = proj[:, ATTN_WIDTH + 2 * KV_WIDTH + CONV_WIDTH:]
    qs = _head_norm_rope(q, gq_ref[...], bd_ref[...], cos, sin)
    q_ref[...] = (jnp.concatenate(qs, axis=-1) * ATTN_SCALE).astype(BF16)
    (kr,) = _head_norm_rope(k, gk_ref[...], bd_ref[:KV_WIDTH, :KV_WIDTH], cos, sin)
    kk_ref[...] = _kv_variants(kr)
    vv_ref[...] = _kv_variants(v)
    a_ref[...] = u * jax.nn.sigmoid(g)


def _kvproj_kernel(x_ref, mod_ref, w_ref, gk_ref, cos_ref, sin_ref, bd_ref, kk_ref, vv_ref):
    h = (x_ref[...] * (1.0 + mod_ref[1:2, :]) + mod_ref[0:1, :]).astype(BF16)
    proj = jnp.dot(h, w_ref[...], preferred_element_type=F32)
    (kr,) = _head_norm_rope(proj[:, :KV_WIDTH], gk_ref[...], bd_ref[...], cos_ref[...], sin_ref[...])
    kk_ref[...] = _kv_variants(kr)
    vv_ref[...] = _kv_variants(proj[:, KV_WIDTH:])


def _seg_spec(tm, width):
    return pl.BlockSpec((None, tm, width), lambda b, t: (b, t, 0))


def _mod_spec(mod_row):
    if mod_row is None:
        return pl.BlockSpec((None, 6, D_MODEL), lambda b, t: (b, 0, 0))
    return pl.BlockSpec((None, 6, D_MODEL), lambda b, t: (mod_row, 0, 0))


def _params(n_axes=2):
    return pltpu.CompilerParams(dimension_semantics=("arbitrary",) * n_axes, vmem_limit_bytes=VMEM_LIMIT)


def _inproj_call(x, mods, w_in, gq, gk, cos, sin, bd, *, tm, mod_row):
    n_seg, seg_len, _ = x.shape
    nt = seg_len // tm
    tab_spec = pl.BlockSpec((tm, LANES), lambda b, t: (t, 0))
    return pl.pallas_call(
        _inproj_kernel,
        grid=(n_seg, nt),
        in_specs=[_seg_spec(tm, D_MODEL), _mod_spec(mod_row), _const_spec((D_MODEL, IN_WIDTH)),
                  _const_spec((1, ATTN_WIDTH)), _const_spec((1, KV_WIDTH)), tab_spec, tab_spec,
                  _const_spec((ATTN_WIDTH, ATTN_WIDTH))],
        out_specs=[_seg_spec(tm, ATTN_WIDTH), _seg_spec(tm, 4 * LANES), _seg_spec(tm, 4 * LANES),
                   _seg_spec(tm, CONV_WIDTH)],
        out_shape=[jax.ShapeDtypeStruct((n_seg, seg_len, ATTN_WIDTH), BF16),
                   jax.ShapeDtypeStruct((n_seg, seg_len, 4 * LANES), BF16),
                   jax.ShapeDtypeStruct((n_seg, seg_len, 4 * LANES), BF16),
                   jax.ShapeDtypeStruct((n_seg, seg_len, CONV_WIDTH), F32)],
        compiler_params=_params(),
        name="in_proj",
    )(x, mods, w_in, gq, gk, cos, sin, bd)


def _kvproj_call(x, mods, w_kv, gk, cos, sin, bd_kv, *, tm, mod_row):
    n_seg, seg_len, _ = x.shape
    nt = seg_len // tm
    tab_spec = pl.BlockSpec((tm, LANES), lambda b, t: (t, 0))
    return pl.pallas_call(
        _kvproj_kernel,
        grid=(n_seg, nt),
        in_specs=[_seg_spec(tm, D_MODEL), _mod_spec(mod_row), _const_spec((D_MODEL, 2 * KV_WIDTH)),
                  _const_spec((1, KV_WIDTH)), tab_spec, tab_spec, _const_spec((KV_WIDTH, KV_WIDTH))],
        out_specs=[_seg_spec(tm, 4 * LANES), _seg_spec(tm, 4 * LANES)],
        out_shape=[jax.ShapeDtypeStruct((n_seg, seg_len, 4 * LANES), BF16),
                   jax.ShapeDtypeStruct((n_seg, seg_len, 4 * LANES), BF16)],
        compiler_params=_params(),
        name="kv_proj",
    )(x, mods, w_kv, gk, cos, sin, bd_kv)


def _attn_kernel(q_ref, *refs, n_src):
    o_ref = refs[2 * n_src]
    for j in range(ATTN_WIDTH // LANES):
        kvh = j // 2
        qp = q_ref[:, j * LANES:(j + 1) * LANES]
        out = None
        for par in range(2):
            c0 = (2 * kvh + par) * LANES
            scores = [lax.dot_general(qp, refs[2 * i][:, c0:c0 + LANES], NT_DIMS, preferred_element_type=F32)
                      for i in range(n_src)]
            m = scores[0].max(axis=-1, keepdims=True)
            for s in scores[1:]:
                m = jnp.maximum(m, s.max(axis=-1, keepdims=True))
            probs = [jnp.exp(s - m) for s in scores]
            denom = probs[0].sum(axis=-1, keepdims=True)
            for p in probs[1:]:
                denom = denom + p.sum(axis=-1, keepdims=True)
            o = jnp.dot(probs[0].astype(BF16), refs[1][:, c0:c0 + LANES], preferred_element_type=F32)
            for i in range(1, n_src):
                o = o + jnp.dot(probs[i].astype(BF16), refs[2 * i + 1][:, c0:c0 + LANES],
                                preferred_element_type=F32)
            o = o * (1.0 / denom)
            out = o if out is None else out + o
        o_ref[:, j * LANES:(j + 1) * LANES] = out.astype(BF16)


def _attn_call(q, sources, *, tq):
    n_seg, seg_len, _ = q.shape
    kv_specs = [pl.BlockSpec((None, s.shape[1], 4 * LANES), lambda b, t: (b, 0, 0)) for s in sources]
    return pl.pallas_call(
        functools.partial(_attn_kernel, n_src=len(sources) // 2),
        grid=(n_seg, seg_len // tq),
        in_specs=[_seg_spec(tq, ATTN_WIDTH)] + kv_specs,
        out_specs=_seg_spec(tq, ATTN_WIDTH),
        out_shape=jax.ShapeDtypeStruct((n_seg, seg_len, ATTN_WIDTH), BF16),
        compiler_params=_params(),
        name="attention",
    )(q, *sources)


def _mixout_kernel(a_ref, o_ref, x_ref, mod_ref, dww_ref, dwb_ref, cg_ref, cb_ref, wout_ref,
                   lg_ref, lb_ref, xo_ref, pad_ref, y_ref, *, tm, nt):
    t = pl.program_id(1)
    seg_len = tm * nt
    start = pl.multiple_of(t * tm, tm)
    zero_halo = jnp.zeros((HALO, CONV_WIDTH), F32)
    if nt == 1:
        top, bot = zero_halo, zero_halo
    else:
        top_start = pl.multiple_of(jnp.maximum(start - HALO, 0), HALO)
        bot_start = pl.multiple_of(jnp.minimum(start + tm, seg_len - HALO), HALO)
        top = jnp.where(t == 0, zero_halo, a_ref[pl.ds(top_start, HALO), :])
        bot = jnp.where(t == nt - 1, zero_halo, a_ref[pl.ds(bot_start, HALO), :])
    pad_ref[0:HALO, :] = top
    pad_ref[HALO:HALO + tm, :] = a_ref[pl.ds(start, tm), :]
    pad_ref[HALO + tm:, :] = bot

    bias = jnp.broadcast_to(dwb_ref[...], (CONV_ROWS, CONV_WIDTH))
    for r0 in range(0, tm, CONV_ROWS):
        acc = bias
        for j in range(CONV_KERNEL):
            off = r0 + HALO - CONV_HALF + j
            acc = acc + pad_ref[off:off + CONV_ROWS, :] * dww_ref[j:j + 1, :]
        y_ref[r0:r0 + CONV_ROWS, :] = acc

    yn = _layer_norm(y_ref[...], cg_ref[...], cb_ref[...])
    conv = (yn * jax.nn.sigmoid(yn)).astype(BF16)
    mixed = jnp.concatenate([o_ref[...], conv], axis=-1)
    y = jnp.dot(mixed, wout_ref[...], preferred_element_type=F32)
    z = DN_ALPHA * x_ref[...] + mod_ref[2:3, :] * y
    xo_ref[...] = _layer_norm(z, lg_ref[...], lb_ref[...])


def _mixout_call(a, o, x, mods, dww, dwb, cg, cb, w_out, lg, lb, *, tm, mod_row):
    n_seg, seg_len, _ = x.shape
    nt = seg_len // tm
    return pl.pallas_call(
        functools.partial(_mixout_kernel, tm=tm, nt=nt),
        grid=(n_seg, nt),
        in_specs=[pl.BlockSpec((None, seg_len, CONV_WIDTH), lambda b, t: (b, 0, 0)),
                  _seg_spec(tm, ATTN_WIDTH), _seg_spec(tm, D_MODEL), _mod_spec(mod_row),
                  _const_spec((CONV_KERNEL, CONV_WIDTH)), _const_spec((1, CONV_WIDTH)),
                  _const_spec((1, CONV_WIDTH)), _const_spec((1, CONV_WIDTH)),
                  _const_spec((D_MODEL, D_MODEL)), _const_spec((1, D_MODEL)), _const_spec((1, D_MODEL))],
        out_specs=_seg_spec(tm, D_MODEL),
        out_shape=jax.ShapeDtypeStruct((n_seg, seg_len, D_MODEL), F32),
        scratch_shapes=[pltpu.VMEM((tm + 2 * HALO, CONV_WIDTH), F32), pltpu.VMEM((tm, CONV_WIDTH), F32)],
        compiler_params=_params(),
        name="mix_out",
    )(a, o, x, mods, dww, dwb, cg, cb, w_out, lg, lb)


def _ffn_kernel(x_ref, mod_ref, w1_ref, w3_ref, w2_ref, lg_ref, lb_ref, xo_ref):
    x = x_ref[...]
    h = (x * (1.0 + mod_ref[4:5, :]) + mod_ref[3:4, :]).astype(BF16)
    y = None
    for c0, width in FF_CHUNKS:
        a1 = jnp.dot(h, w1_ref[:, c0:c0 + width], preferred_element_type=F32)
        a3 = jnp.dot(h, w3_ref[:, c0:c0 + width], preferred_element_type=F32)
        hid = (a1 * jax.nn.sigmoid(a1) * a3).astype(BF16)
        part = jnp.dot(hid, w2_ref[c0:c0 + width, :], preferred_element_type=F32)
        y = part if y is None else y + part
    z = DN_ALPHA * x + mod_ref[5:6, :] * y
    xo_ref[...] = _layer_norm(z, lg_ref[...], lb_ref[...])


def _ffn_call(x, mods, w1, w3, w2, lg, lb, *, tm, mod_row):
    n_seg, seg_len, _ = x.shape
    return pl.pallas_call(
        _ffn_kernel,
        grid=(n_seg, seg_len // tm),
        in_specs=[_seg_spec(tm, D_MODEL), _mod_spec(mod_row), _const_spec((D_MODEL, D_FF)),
                  _const_spec((D_MODEL, D_FF)), _const_spec((D_FF, D_MODEL)),
                  _const_spec((1, D_MODEL)), _const_spec((1, D_MODEL))],
        out_specs=_seg_spec(tm, D_MODEL),
        out_shape=jax.ShapeDtypeStruct((n_seg, seg_len, D_MODEL), F32),
        compiler_params=_params(),
        name="ffn",
    )(x, mods, w1, w3, w2, lg, lb)


def _rope_tables(n):
    rows = n // GRID_W
    r, col = jnp.meshgrid(jnp.arange(rows), jnp.arange(GRID_W), indexing="ij")
    pos = jnp.stack([r.reshape(-1), col.reshape(-1)], axis=-1).astype(F32)
    n_freq = HEAD_DIM // 4
    freqs = ROPE_THETA ** (-jnp.arange(n_freq, dtype=F32) / n_freq)
    ang = pos[:, :, None] * freqs
    cos, sin = jnp.cos(ang), jnp.sin(ang)
    cos_head = jnp.concatenate([cos[:, 0], cos[:, 0], cos[:, 1], cos[:, 1]], axis=-1)
    sin_head = jnp.concatenate([-sin[:, 0], sin[:, 0], -sin[:, 1], sin[:, 1]], axis=-1)
    reps = LANES // HEAD_DIM
    return jnp.tile(cos_head, (1, reps)), jnp.tile(sin_head, (1, reps))


def kernel(x, c, ctx, c_ctx, w_ada, b_ada, w_in, q_norm_g, k_norm_g, dw_w, dw_b, conv_ln_g, conv_ln_b,
           w_out, ln1_g, ln1_b, w_ff1, w_ff3, w_ff2, ln2_g, ln2_b):
    n_batch, n_lat, _ = x.shape
    n_ctx = ctx.shape[1]
    assert n_batch + 1 <= MOD_ROWS
    ctx_row = n_batch
    lat_tm, ctx_tm = 512, n_ctx

    c_all = jnp.concatenate(
        [c, c_ctx[None, :], jnp.zeros((MOD_ROWS - n_batch - 1, D_MODEL), F32)], axis=0)
    mods = _ada_call(c_all, w_ada, b_ada.reshape(DEPTH, 1, 6 * D_MODEL))
    mods = mods.reshape(DEPTH, MOD_ROWS, 6, D_MODEL)

    cos_l, sin_l = _rope_tables(n_lat)
    cos_c, sin_c = jnp.ones((n_ctx, LANES), F32), jnp.zeros((n_ctx, LANES), F32)
    head_id = jnp.arange(ATTN_WIDTH) // HEAD_DIM
    bd = (head_id[:, None] == head_id[None, :]).astype(BF16)
    bd_kv = bd[:KV_WIDTH, :KV_WIDTH]

    w_in_b, w_out_b = w_in.astype(BF16), w_out.astype(BF16)
    w1_b, w3_b, w2_b = w_ff1.astype(BF16), w_ff3.astype(BF16), w_ff2.astype(BF16)
    gq = jnp.tile(q_norm_g, (1, N_Q_HEADS)).reshape(DEPTH, 1, ATTN_WIDTH)
    gk = jnp.tile(k_norm_g, (1, N_KV_HEADS)).reshape(DEPTH, 1, KV_WIDTH)
    dww = dw_w.reshape(DEPTH, CONV_KERNEL, CONV_WIDTH)
    row = lambda p, l: p[l].reshape(1, -1)

    for l in range(DEPTH):
        last = l == DEPTH - 1
        q_l, kk_l, vv_l, a_l = _inproj_call(x, mods[l], w_in_b[l], gq[l], gk[l], cos_l, sin_l, bd,
                                            tm=lat_tm, mod_row=None)
        if last:
            kk_c, vv_c = _kvproj_call(ctx, mods[l], w_in_b[l][:, ATTN_WIDTH:ATTN_WIDTH + 2 * KV_WIDTH],
                                      gk[l], cos_c, sin_c, bd_kv, tm=ctx_tm, mod_row=ctx_row)
        else:
            q_c, kk_c, vv_c, a_c = _inproj_call(ctx, mods[l], w_in_b[l], gq[l], gk[l], cos_c, sin_c, bd,
                                                tm=ctx_tm, mod_row=ctx_row)
        mix_args = (dww[l], row(dw_b, l), row(conv_ln_g, l), row(conv_ln_b, l), w_out_b[l],
                    row(ln1_g, l), row(ln1_b, l))
        ffn_args = (w1_b[l], w3_b[l], w2_b[l], row(ln2_g, l), row(ln2_b, l))

        o_l = _attn_call(q_l, [kk_c, vv_c, kk_l, vv_l], tq=lat_tm)
        x_mid = _mixout_call(a_l, o_l, x, mods[l], *mix_args, tm=lat_tm, mod_row=None)
        x = _ffn_call(x_mid, mods[l], *ffn_args, tm=lat_tm, mod_row=None)

        if not last:
            o_c = _attn_call(q_c, [kk_c, vv_c], tq=ctx_tm)
            ctx_mid = _mixout_call(a_c, o_c, ctx, mods[l], *mix_args, tm=ctx_tm, mod_row=ctx_row)
            ctx = _ffn_call(ctx_mid, mods[l], *ffn_args, tm=ctx_tm, mod_row=ctx_row)
    return x
```

```python
import functools

import jax
import jax.numpy as jnp
from jax import lax
from jax.experimental import pallas as pl
from jax.experimental.pallas import tpu as pltpu

F32 = jnp.float32
BF16 = jnp.bfloat16

D_MODEL = 1024
DEPTH = 4
GRID_W = 64
HEAD_DIM = 64
N_Q_HEADS = 8
N_KV_HEADS = 2
ATTN_WIDTH = N_Q_HEADS * HEAD_DIM
KV_WIDTH = N_KV_HEADS * HEAD_DIM
CONV_WIDTH = D_MODEL - ATTN_WIDTH
CONV_KERNEL = 31
CONV_HALF = CONV_KERNEL // 2
D_FF = 2816
ROPE_THETA = 10000.0
DN_ALPHA = (2 * DEPTH) ** 0.25
LN_EPS = 1e-6
RMS_EPS = 1e-6
ATTN_SCALE = HEAD_DIM ** -0.5
IN_WIDTH = ATTN_WIDTH + 2 * KV_WIDTH + 2 * CONV_WIDTH

LANES = 128
SUBLANES = 8
HALO = 16
CONV_TILES = 4
MOD_ROWS = 16
VMEM_LIMIT = 52 * 1024 * 1024
FF_CHUNKS = ((0, 768), (768, 768), (1536, 768), (2304, 512))

NT_DIMS = (((1,), (1,)), ((), ()))


def _const_spec(shape):
    zeros = (0,) * len(shape)
    return pl.BlockSpec(shape, lambda *_: zeros, pipeline_mode=pl.Buffered(1))


def _layer_norm(z, g, b):
    mu = jnp.mean(z, axis=-1, keepdims=True)
    zc = z - mu
    var = jnp.mean(zc * zc, axis=-1, keepdims=True)
    return zc * lax.rsqrt(var + LN_EPS) * g + b


def _params(n_axes=2):
    return pltpu.CompilerParams(dimension_semantics=("arbitrary",) * n_axes, vmem_limit_bytes=VMEM_LIMIT)


def _ada_kernel(c_ref, w_ref, b_ref, o_ref):
    c = c_ref[...]
    s = (c * jax.nn.sigmoid(c)).astype(BF16)
    o_ref[...] = jnp.dot(s, w_ref[...].astype(BF16), preferred_element_type=F32) + b_ref[...]


def _ada_call(c_all, w_ada, b_ada):
    n_col = 6 * D_MODEL // D_MODEL
    return pl.pallas_call(
        _ada_kernel,
        grid=(DEPTH, n_col),
        in_specs=[
            pl.BlockSpec((MOD_ROWS, D_MODEL), lambda l, j: (0, 0)),
            pl.BlockSpec((None, D_MODEL, D_MODEL), lambda l, j: (l, 0, j)),
            pl.BlockSpec((None, 1, D_MODEL), lambda l, j: (l, 0, j)),
        ],
        out_specs=pl.BlockSpec((None, MOD_ROWS, D_MODEL), lambda l, j: (l, 0, j)),
        out_shape=jax.ShapeDtypeStruct((DEPTH, MOD_ROWS, 6 * D_MODEL), F32),
        compiler_params=_params(),
        name="ada_mod",
    )(c_all, w_ada, b_ada)


def _head_norm_rope(z, gain, bd, cos, sin):
    ssq = jnp.dot((z * z).astype(BF16), bd, preferred_element_type=F32)
    zn = z * lax.rsqrt(ssq * (1.0 / HEAD_DIM) + RMS_EPS) * gain
    outs = []
    for c0 in range(0, z.shape[1], LANES):
        zc = zn[:, c0:c0 + LANES]
        up = pltpu.roll(zc, LANES - 16, 1)
        dn = pltpu.roll(zc, 16, 1)
        lane = lax.broadcasted_iota(jnp.int32, zc.shape, 1)
        partner = jnp.where((lane & 16) == 0, up, dn)
        outs.append(zc * cos + partner * sin)
    return outs


def _kv_variants(z):
    zs = pltpu.roll(z, HEAD_DIM, 1)
    lo = lax.broadcasted_iota(jnp.int32, z.shape, 1) < HEAD_DIM
    zero = jnp.zeros_like(z)
    return jnp.concatenate(
        [jnp.where(lo, z, zero), jnp.where(lo, zero, zs), jnp.where(lo, zs, zero), jnp.where(lo, zero, z)],
        axis=-1).astype(BF16)


def _inproj_kernel(x_ref, mod_ref, w_ref, gq_ref, gk_ref, cos_ref, sin_ref, bd_ref,
                   q_ref, kk_ref, vv_ref, a_ref):
    h = (x_ref[...] * (1.0 + mod_ref[1:2, :]) + mod_ref[0:1, :]).astype(BF16)
    proj = jnp.dot(h, w_ref[...], preferred_element_type=F32)
    cos, sin = cos_ref[...], sin_ref[...]
    q = proj[:, :ATTN_WIDTH]
    k = proj[:, ATTN_WIDTH:ATTN_WIDTH + KV_WIDTH]
    v = proj[:, ATTN_WIDTH + KV_WIDTH:ATTN_WIDTH + 2 * KV_WIDTH]
    u = proj[:, ATTN_WIDTH + 2 * KV_WIDTH:ATTN_WIDTH + 2 * KV_WIDTH + CONV_WIDTH]
    g = proj[:, ATTN_WIDTH + 2 * KV_WIDTH + CONV_WIDTH:]
    qs = _head_norm_rope(q, gq_ref[...], bd_ref[...], cos, sin)
    q_ref[...] = (jnp.concatenate(qs, axis=-1) * ATTN_SCALE).astype(BF16)
    (kr,) = _head_norm_rope(k, gk_ref[...], bd_ref[:KV_WIDTH, :KV_WIDTH], cos, sin)
    kk_ref[...] = _kv_variants(kr)
    vv_ref[...] = _kv_variants(v)
    a_ref[...] = (u * jax.nn.sigmoid(g)).astype(BF16)


def _kvproj_kernel(x_ref, mod_ref, w_ref, gk_ref, cos_ref, sin_ref, bd_ref, kk_ref, vv_ref):
    h = (x_ref[...] * (1.0 + mod_ref[1:2, :]) + mod_ref[0:1, :]).astype(BF16)
    proj = jnp.dot(h, w_ref[...], preferred_element_type=F32)
    (kr,) = _head_norm_rope(proj[:, :KV_WIDTH], gk_ref[...], bd_ref[...], cos_ref[...], sin_ref[...])
    kk_ref[...] = _kv_variants(kr)
    vv_ref[...] = _kv_variants(proj[:, KV_WIDTH:])


def _seg_spec(tm, width):
    return pl.BlockSpec((None, tm, width), lambda b, t: (b, t, 0))


def _mod_spec(mod_row):
    if mod_row is None:
        return pl.BlockSpec((None, 6, D_MODEL), lambda b, t: (b, 0, 0))
    return pl.BlockSpec((None, 6, D_MODEL), lambda b, t: (mod_row, 0, 0))


def _inproj_call(x, mods, w_in, gq, gk, cos, sin, bd, *, tm, mod_row):
    n_seg, seg_len, _ = x.shape
    nt = seg_len // tm
    tab_spec = pl.BlockSpec((tm, LANES), lambda b, t: (t, 0))
    return pl.pallas_call(
        _inproj_kernel,
        grid=(n_seg, nt),
        in_specs=[_seg_spec(tm, D_MODEL), _mod_spec(mod_row), _const_spec((D_MODEL, IN_WIDTH)),
                  _const_spec((1, ATTN_WIDTH)), _const_spec((1, KV_WIDTH)), tab_spec, tab_spec,
                  _const_spec((ATTN_WIDTH, ATTN_WIDTH))],
        out_specs=[_seg_spec(tm, ATTN_WIDTH), _seg_spec(tm, 4 * LANES), _seg_spec(tm, 4 * LANES),
                   _seg_spec(tm, CONV_WIDTH)],
        out_shape=[jax.ShapeDtypeStruct((n_seg, seg_len, ATTN_WIDTH), BF16),
                   jax.ShapeDtypeStruct((n_seg, seg_len, 4 * LANES), BF16),
                   jax.ShapeDtypeStruct((n_seg, seg_len, 4 * LANES), BF16),
                   jax.ShapeDtypeStruct((n_seg, seg_len, CONV_WIDTH), BF16)],
        compiler_params=_params(),
        name="in_proj",
    )(x, mods, w_in, gq, gk, cos, sin, bd)


def _kvproj_call(x, mods, w_kv, gk, cos, sin, bd_kv, *, tm, mod_row):
    n_seg, seg_len, _ = x.shape
    nt = seg_len // tm
    tab_spec = pl.BlockSpec((tm, LANES), lambda b, t: (t, 0))
    return pl.pallas_call(
        _kvproj_kernel,
        grid=(n_seg, nt),
        in_specs=[_seg_spec(tm, D_MODEL), _mod_spec(mod_row), _const_spec((D_MODEL, 2 * KV_WIDTH)),
                  _const_spec((1, KV_WIDTH)), tab_spec, tab_spec, _const_spec((KV_WIDTH, KV_WIDTH))],
        out_specs=[_seg_spec(tm, 4 * LANES), _seg_spec(tm, 4 * LANES)],
        out_shape=[jax.ShapeDtypeStruct((n_seg, seg_len, 4 * LANES), BF16),
                   jax.ShapeDtypeStruct((n_seg, seg_len, 4 * LANES), BF16)],
        compiler_params=_params(),
        name="kv_proj",
    )(x, mods, w_kv, gk, cos, sin, bd_kv)


def _attn_kernel(q_ref, *refs, n_src):
    o_ref = refs[2 * n_src]
    for j in range(ATTN_WIDTH // LANES):
        kvh = j // 2
        qp = q_ref[:, j * LANES:(j + 1) * LANES]
        out = None
        for par in range(2):
            c0 = (2 * kvh + par) * LANES
            scores = [lax.dot_general(qp, refs[2 * i][:, c0:c0 + LANES], NT_DIMS, preferred_element_type=F32)
                      for i in range(n_src)]
            m = scores[0].max(axis=-1, keepdims=True)
            for s in scores[1:]:
                m = jnp.maximum(m, s.max(axis=-1, keepdims=True))
            probs = [jnp.exp(s - m) for s in scores]
            denom = probs[0].sum(axis=-1, keepdims=True)
            for p in probs[1:]:
                denom = denom + p.sum(axis=-1, keepdims=True)
            o = jnp.dot(probs[0].astype(BF16), refs[1][:, c0:c0 + LANES], preferred_element_type=F32)
            for i in range(1, n_src):
                o = o + jnp.dot(probs[i].astype(BF16), refs[2 * i + 1][:, c0:c0 + LANES],
                                preferred_element_type=F32)
            o = o * (1.0 / denom)
            out = o if out is None else out + o
        o_ref[:, j * LANES:(j + 1) * LANES] = out.astype(BF16)


def _attn_call(q, sources, *, tq):
    n_seg, seg_len, _ = q.shape
    kv_specs = [pl.BlockSpec((None, s.shape[1], 4 * LANES), lambda b, t: (b, 0, 0)) for s in sources]
    return pl.pallas_call(
        functools.partial(_attn_kernel, n_src=len(sources) // 2),
        grid=(n_seg, seg_len // tq),
        in_specs=[_seg_spec(tq, ATTN_WIDTH)] + kv_specs,
        out_specs=_seg_spec(tq, ATTN_WIDTH),
        out_shape=jax.ShapeDtypeStruct((n_seg, seg_len, ATTN_WIDTH), BF16),
        compiler_params=_params(),
        name="attention",
    )(q, *sources)


def _conv_stage(a_ref, t, pad_ref, *, tm, nt):
    seg_len = tm * nt
    zero_halo = jnp.zeros((HALO, CONV_WIDTH), F32)
    if nt == 1:
        start = 0
        top, bot = zero_halo, zero_halo
    else:
        start = pl.multiple_of(t * tm, tm)
        top_start = pl.multiple_of(jnp.maximum(start - HALO, 0), HALO)
        bot_start = pl.multiple_of(jnp.minimum(start + tm, seg_len - HALO), HALO)
        top = jnp.where(t == 0, zero_halo, a_ref[pl.ds(top_start, HALO), :].astype(F32))
        bot = jnp.where(t == nt - 1, zero_halo, a_ref[pl.ds(bot_start, HALO), :].astype(F32))
    halo_tiles = HALO // SUBLANES
    row_tiles = tm // SUBLANES
    pad_ref[0:halo_tiles] = top.reshape(halo_tiles, SUBLANES, CONV_WIDTH)
    pad_ref[halo_tiles:halo_tiles + row_tiles] = (
        a_ref[pl.ds(start, tm), :].astype(F32).reshape(row_tiles, SUBLANES, CONV_WIDTH))
    pad_ref[halo_tiles + row_tiles:] = bot.reshape(halo_tiles, SUBLANES, CONV_WIDTH)


def _conv_column(c0, dww_ref, dwb_ref, pad_ref, shift_ref, y_ref):
    n_tiles = pad_ref.shape[0]
    row_tiles = y_ref.shape[0]
    sub = lax.broadcasted_iota(jnp.int32, (n_tiles - 1, SUBLANES, LANES), 1)
    for s in range(1, SUBLANES):
        rot = pltpu.roll(pad_ref[:, :, c0:c0 + LANES], SUBLANES - s, 1)
        shift_ref[s - 1] = jnp.where(sub < SUBLANES - s, rot[:-1], rot[1:])
    for k0 in range(0, row_tiles, CONV_TILES):
        acc = jnp.broadcast_to(dwb_ref[:, c0:c0 + LANES], (CONV_TILES, SUBLANES, LANES))
        for j in range(CONV_KERNEL):
            m, s = divmod(HALO - CONV_HALF + j, SUBLANES)
            if s == 0:
                src = pad_ref[k0 + m:k0 + m + CONV_TILES, :, c0:c0 + LANES]
            else:
                src = shift_ref[s - 1, k0 + m:k0 + m + CONV_TILES]
            acc = acc + src * dww_ref[j, :, c0:c0 + LANES]
        y_ref[k0:k0 + CONV_TILES, :, c0:c0 + LANES] = acc


def _mixffn_kernel(run_ref, a_ref, o_ref, x_ref, mod_ref, dww_ref, dwb_ref, cg_ref, cb_ref, wout_ref,
                   lg1_ref, lb1_ref, w1_ref, w3_ref, w2_ref, lg2_ref, lb2_ref, xo_ref,
                   conv_ref, pad_ref, shift_ref, y_ref, xm_ref, h_ref, yf_ref, *, tm, nt):
    i = pl.program_id(0)
    n_steps = pl.num_programs(0)
    phase = run_ref[0] != 0
    t = jnp.minimum(i, n_steps - 2) % nt
    assert len(FF_CHUNKS) == CONV_WIDTH // LANES

    @pl.when(i == 0)
    def _():
        conv_ref[1] = jnp.zeros((tm, CONV_WIDTH), BF16)

    @pl.when(phase)
    def _():
        mixed = jnp.concatenate([o_ref[...], conv_ref[(i + 1) % 2]], axis=-1)
        y = jnp.dot(mixed, wout_ref[...], preferred_element_type=F32)
        xm = _layer_norm(DN_ALPHA * x_ref[...] + mod_ref[2:3, :] * y, lg1_ref[...], lb1_ref[...])
        xm_ref[...] = xm
        h_ref[...] = (xm * (1.0 + mod_ref[4:5, :]) + mod_ref[3:4, :]).astype(BF16)
        _conv_stage(a_ref, t, pad_ref, tm=tm, nt=nt)

    for k, (c0, width) in enumerate(FF_CHUNKS):
        @pl.when(phase)
        def _(k=k, c0=c0, width=width):
            h = h_ref[...]
            a1 = jnp.dot(h, w1_ref[:, c0:c0 + width], preferred_element_type=F32)
            a3 = jnp.dot(h, w3_ref[:, c0:c0 + width], preferred_element_type=F32)
            hid = (a1 * jax.nn.sigmoid(a1) * a3).astype(BF16)
            part = jnp.dot(hid, w2_ref[c0:c0 + width, :], preferred_element_type=F32)
            if k == 0:
                yf_ref[...] = part
            else:
                yf_ref[...] += part
            _conv_column(k * LANES, dww_ref, dwb_ref, pad_ref, shift_ref, y_ref)

    @pl.when(phase)
    def _():
        z = DN_ALPHA * xm_ref[...] + mod_ref[5:6, :] * yf_ref[...]
        xo_ref[...] = _layer_norm(z, lg2_ref[...], lb2_ref[...])
        yn = _layer_norm(y_ref[...].reshape(tm, CONV_WIDTH), cg_ref[...], cb_ref[...])
        conv_ref[i % 2] = (yn * jax.nn.sigmoid(yn)).astype(BF16)


def _mixffn_call(a, o, x, mods, dww8, dwb, cg, cb, w_out, lg1, lb1, w1, w3, w2, lg2, lb2, *, tm, mod_row):
    n_seg, seg_len, _ = x.shape
    nt = seg_len // tm
    n_total = n_seg * nt
    prev = lambda i: jnp.maximum(i - 1, 0)
    row_spec = lambda width: pl.BlockSpec((tm, width), lambda i: (prev(i), 0))
    if mod_row is None:
        mod_spec = pl.BlockSpec((None, 6, D_MODEL), lambda i: (prev(i) // nt, 0, 0))
    else:
        mod_spec = pl.BlockSpec((None, 6, D_MODEL), lambda i: (mod_row, 0, 0))
    n_win = (tm + 2 * HALO) // SUBLANES
    out = pl.pallas_call(
        functools.partial(_mixffn_kernel, tm=tm, nt=nt),
        grid=(n_total + 1,),
        in_specs=[pl.BlockSpec(memory_space=pltpu.SMEM),
                  pl.BlockSpec((None, seg_len, CONV_WIDTH), lambda i: (jnp.minimum(i, n_total - 1) // nt, 0, 0)),
                  row_spec(ATTN_WIDTH), row_spec(D_MODEL), mod_spec,
                  _const_spec((CONV_KERNEL, SUBLANES, CONV_WIDTH)), _const_spec((1, CONV_WIDTH)),
                  _const_spec((1, CONV_WIDTH)), _const_spec((1, CONV_WIDTH)),
                  _const_spec((D_MODEL, D_MODEL)), _const_spec((1, D_MODEL)), _const_spec((1, D_MODEL)),
                  _const_spec((D_MODEL, D_FF)), _const_spec((D_MODEL, D_FF)), _const_spec((D_FF, D_MODEL)),
                  _const_spec((1, D_MODEL)), _const_spec((1, D_MODEL))],
        out_specs=row_spec(D_MODEL),
        out_shape=jax.ShapeDtypeStruct((n_seg * seg_len, D_MODEL), F32),
        scratch_shapes=[pltpu.VMEM((2, tm, CONV_WIDTH), BF16),
                        pltpu.VMEM((n_win, SUBLANES, CONV_WIDTH), F32),
                        pltpu.VMEM((SUBLANES - 1, n_win - 1, SUBLANES, LANES), F32),
                        pltpu.VMEM((tm // SUBLANES, SUBLANES, CONV_WIDTH), F32),
                        pltpu.VMEM((tm, D_MODEL), F32), pltpu.VMEM((tm, D_MODEL), BF16),
                        pltpu.VMEM((tm, D_MODEL), F32)],
        compiler_params=_params(1),
        name="mix_ffn",
    )(jnp.ones((1,), jnp.int32), a, o.reshape(n_seg * seg_len, ATTN_WIDTH), x.reshape(n_seg * seg_len, D_MODEL),
      mods, dww8, dwb, cg, cb,
      w_out, lg1, lb1, w1, w3, w2, lg2, lb2)
    return out.reshape(n_seg, seg_len, D_MODEL)


def _rope_tables(n):
    rows = n // GRID_W
    r, col = jnp.meshgrid(jnp.arange(rows), jnp.arange(GRID_W), indexing="ij")
    pos = jnp.stack([r.reshape(-1), col.reshape(-1)], axis=-1).astype(F32)
    n_freq = HEAD_DIM // 4
    freqs = ROPE_THETA ** (-jnp.arange(n_freq, dtype=F32) / n_freq)
    ang = pos[:, :, None] * freqs
    cos, sin = jnp.cos(ang), jnp.sin(ang)
    cos_head = jnp.concatenate([cos[:, 0], cos[:, 0], cos[:, 1], cos[:, 1]], axis=-1)
    sin_head = jnp.concatenate([-sin[:, 0], sin[:, 0], -sin[:, 1], sin[:, 1]], axis=-1)
    reps = LANES // HEAD_DIM
    return jnp.tile(cos_head, (1, reps)), jnp.tile(sin_head, (1, reps))


def kernel(x, c, ctx, c_ctx, w_ada, b_ada, w_in, q_norm_g, k_norm_g, dw_w, dw_b, conv_ln_g, conv_ln_b,
           w_out, ln1_g, ln1_b, w_ff1, w_ff3, w_ff2, ln2_g, ln2_b):
    n_batch, n_lat, _ = x.shape
    n_ctx = ctx.shape[1]
    assert n_batch + 1 <= MOD_ROWS
    ctx_row = n_batch
    lat_tm, ctx_tm = 512, n_ctx

    c_all = jnp.concatenate(
        [c, c_ctx[None, :], jnp.zeros((MOD_ROWS - n_batch - 1, D_MODEL), F32)], axis=0)
    mods = _ada_call(c_all, w_ada, b_ada.reshape(DEPTH, 1, 6 * D_MODEL))
    mods = mods.reshape(DEPTH, MOD_ROWS, 6, D_MODEL)

    cos_l, sin_l = _rope_tables(n_lat)
    cos_c, sin_c = jnp.ones((n_ctx, LANES), F32), jnp.zeros((n_ctx, LANES), F32)
    head_id = jnp.arange(ATTN_WIDTH) // HEAD_DIM
    bd = (head_id[:, None] == head_id[None, :]).astype(BF16)
    bd_kv = bd[:KV_WIDTH, :KV_WIDTH]

    w_in_b, w_out_b = w_in.astype(BF16), w_out.astype(BF16)
    w1_b, w3_b, w2_b = w_ff1.astype(BF16), w_ff3.astype(BF16), w_ff2.astype(BF16)
    gq = jnp.tile(q_norm_g, (1, N_Q_HEADS)).reshape(DEPTH, 1, ATTN_WIDTH)
    gk = jnp.tile(k_norm_g, (1, N_KV_HEADS)).reshape(DEPTH, 1, KV_WIDTH)
    dww8 = jnp.broadcast_to(dw_w.reshape(DEPTH, CONV_KERNEL, 1, CONV_WIDTH),
                            (DEPTH, CONV_KERNEL, SUBLANES, CONV_WIDTH))
    row = lambda p, l: p[l].reshape(1, -1)

    for l in range(DEPTH):
        last = l == DEPTH - 1
        q_l, kk_l, vv_l, a_l = _inproj_call(x, mods[l], w_in_b[l], gq[l], gk[l], cos_l, sin_l, bd,
                                            tm=lat_tm, mod_row=None)
        if last:
            kk_c, vv_c = _kvproj_call(ctx, mods[l], w_in_b[l][:, ATTN_WIDTH:ATTN_WIDTH + 2 * KV_WIDTH],
                                      gk[l], cos_c, sin_c, bd_kv, tm=ctx_tm, mod_row=ctx_row)
        else:
            q_c, kk_c, vv_c, a_c = _inproj_call(ctx, mods[l], w_in_b[l], gq[l], gk[l], cos_c, sin_c, bd,
                                                tm=ctx_tm, mod_row=ctx_row)
        tail_args = (dww8[l], row(dw_b, l), row(conv_ln_g, l), row(conv_ln_b, l), w_out_b[l],
                     row(ln1_g, l), row(ln1_b, l), w1_b[l], w3_b[l], w2_b[l], row(ln2_g, l), row(ln2_b, l))

        o_l = _attn_call(q_l, [kk_c, vv_c, kk_l, vv_l], tq=lat_tm)
        x = _mixffn_call(a_l, o_l, x, mods[l], *tail_args, tm=lat_tm, mod_row=None)

        if not last:
            o_c = _attn_call(q_c, [kk_c, vv_c], tq=ctx_tm)
            ctx = _mixffn_call(a_c, o_c, ctx, mods[l], *tail_args, tm=ctx_tm, mod_row=ctx_row)
    return x
```

```python
import functools

import jax
import jax.numpy as jnp
from jax import lax
from jax.experimental import pallas as pl
from jax.experimental.pallas import tpu as pltpu

F32 = jnp.float32
BF16 = jnp.bfloat16

D_MODEL = 1024
DEPTH = 4
GRID_W = 64
HEAD_DIM = 64
N_Q_HEADS = 8
N_KV_HEADS = 2
ATTN_WIDTH = N_Q_HEADS * HEAD_DIM
KV_WIDTH = N_KV_HEADS * HEAD_DIM
CONV_WIDTH = D_MODEL - ATTN_WIDTH
CONV_KERNEL = 31
CONV_HALF = CONV_KERNEL // 2
D_FF = 2816
ROPE_THETA = 10000.0
DN_ALPHA = (2 * DEPTH) ** 0.25
LN_EPS = 1e-6
RMS_EPS = 1e-6
ATTN_SCALE = HEAD_DIM ** -0.5
IN_WIDTH = ATTN_WIDTH + 2 * KV_WIDTH + 2 * CONV_WIDTH

LANES = 128
SUBLANES = 8
HALO = 16
CONV_TILES = 4
CHAIN_DELAY_ROLLS = 4
MOD_ROWS = 16
VMEM_LIMIT = 52 * 1024 * 1024
FF_CHUNKS = ((0, 768), (768, 768), (1536, 768), (2304, 512))

NT_DIMS = (((1,), (1,)), ((), ()))


def _const_spec(shape):
    zeros = (0,) * len(shape)
    return pl.BlockSpec(shape, lambda *_: zeros, pipeline_mode=pl.Buffered(1))


def _layer_norm(z, g, b):
    mu = jnp.mean(z, axis=-1, keepdims=True)
    zc = z - mu
    var = jnp.mean(zc * zc, axis=-1, keepdims=True)
    return zc * lax.rsqrt(var + LN_EPS) * g + b


def _params(n_axes=2):
    return pltpu.CompilerParams(dimension_semantics=("arbitrary",) * n_axes, vmem_limit_bytes=VMEM_LIMIT)


def _ada_kernel(c_ref, w_ref, b_ref, o_ref):
    c = c_ref[...]
    s = (c * jax.nn.sigmoid(c)).astype(BF16)
    o_ref[...] = jnp.dot(s, w_ref[...].astype(BF16), preferred_element_type=F32) + b_ref[...]


def _ada_call(c_all, w_ada, b_ada):
    n_col = 6 * D_MODEL // D_MODEL
    return pl.pallas_call(
        _ada_kernel,
        grid=(DEPTH, n_col),
        in_specs=[
            pl.BlockSpec((MOD_ROWS, D_MODEL), lambda l, j: (0, 0)),
            pl.BlockSpec((None, D_MODEL, D_MODEL), lambda l, j: (l, 0, j)),
            pl.BlockSpec((None, 1, D_MODEL), lambda l, j: (l, 0, j)),
        ],
        out_specs=pl.BlockSpec((None, MOD_ROWS, D_MODEL), lambda l, j: (l, 0, j)),
        out_shape=jax.ShapeDtypeStruct((DEPTH, MOD_ROWS, 6 * D_MODEL), F32),
        compiler_params=_params(),
        name="ada_mod",
    )(c_all, w_ada, b_ada)


def _head_norm_rope(z, gain, bd, cos, sin):
    ssq = jnp.dot((z * z).astype(BF16), bd, preferred_element_type=F32)
    zn = z * lax.rsqrt(ssq * (1.0 / HEAD_DIM) + RMS_EPS) * gain
    outs = []
    for c0 in range(0, z.shape[1], LANES):
        zc = zn[:, c0:c0 + LANES]
        up = pltpu.roll(zc, LANES - 16, 1)
        dn = pltpu.roll(zc, 16, 1)
        lane = lax.broadcasted_iota(jnp.int32, zc.shape, 1)
        partner = jnp.where((lane & 16) == 0, up, dn)
        outs.append(zc * cos + partner * sin)
    return outs


def _kv_variants(z):
    zs = pltpu.roll(z, HEAD_DIM, 1)
    lo = lax.broadcasted_iota(jnp.int32, z.shape, 1) < HEAD_DIM
    zero = jnp.zeros_like(z)
    return jnp.concatenate(
        [jnp.where(lo, z, zero), jnp.where(lo, zero, zs), jnp.where(lo, zs, zero), jnp.where(lo, zero, z)],
        axis=-1).astype(BF16)


def _inproj_kernel(x_ref, mod_ref, w_ref, gq_ref, gk_ref, cos_ref, sin_ref, bd_ref,
                   q_ref, kk_ref, vv_ref, a_ref):
    h = (x_ref[...] * (1.0 + mod_ref[1:2, :]) + mod_ref[0:1, :]).astype(BF16)
    proj = jnp.dot(h, w_ref[...], preferred_element_type=F32)
    cos, sin = cos_ref[...], sin_ref[...]
    q = proj[:, :ATTN_WIDTH]
    k = proj[:, ATTN_WIDTH:ATTN_WIDTH + KV_WIDTH]
    v = proj[:, ATTN_WIDTH + KV_WIDTH:ATTN_WIDTH + 2 * KV_WIDTH]
    u = proj[:, ATTN_WIDTH + 2 * KV_WIDTH:ATTN_WIDTH + 2 * KV_WIDTH + CONV_WIDTH]
    g = proj[:, ATTN_WIDTH + 2 * KV_WIDTH + CONV_WIDTH:]
    qs = _head_norm_rope(q, gq_ref[...], bd_ref[...], cos, sin)
    q_ref[...] = (jnp.concatenate(qs, axis=-1) * ATTN_SCALE).astype(BF16)
    (kr,) = _head_norm_rope(k, gk_ref[...], bd_ref[:KV_WIDTH, :KV_WIDTH], cos, sin)
    kk_ref[...] = _kv_variants(kr)
    vv_ref[...] = _kv_variants(v)
    a_ref[...] = (u * jax.nn.sigmoid(g)).astype(BF16)


def _kvproj_kernel(x_ref, mod_ref, w_ref, gk_ref, cos_ref, sin_ref, bd_ref, kk_ref, vv_ref):
    h = (x_ref[...] * (1.0 + mod_ref[1:2, :]) + mod_ref[0:1, :]).astype(BF16)
    proj = jnp.dot(h, w_ref[...], preferred_element_type=F32)
    (kr,) = _head_norm_rope(proj[:, :KV_WIDTH], gk_ref[...], bd_ref[...], cos_ref[...], sin_ref[...])
    kk_ref[...] = _kv_variants(kr)
    vv_ref[...] = _kv_variants(proj[:, KV_WIDTH:])


def _seg_spec(tm, width):
    return pl.BlockSpec((None, tm, width), lambda b, t: (b, t, 0))


def _mod_spec(mod_row):
    if mod_row is None:
        return pl.BlockSpec((None, 6, D_MODEL), lambda b, t: (b, 0, 0))
    return pl.BlockSpec((None, 6, D_MODEL), lambda b, t: (mod_row, 0, 0))


def _inproj_call(x, mods, w_in, gq, gk, cos, sin, bd, *, tm, mod_row):
    n_seg, seg_len, _ = x.shape
    nt = seg_len // tm
    tab_spec = pl.BlockSpec((tm, LANES), lambda b, t: (t, 0))
    return pl.pallas_call(
        _inproj_kernel,
        grid=(n_seg, nt),
        in_specs=[_seg_spec(tm, D_MODEL), _mod_spec(mod_row), _const_spec((D_MODEL, IN_WIDTH)),
                  _const_spec((1, ATTN_WIDTH)), _const_spec((1, KV_WIDTH)), tab_spec, tab_spec,
                  _const_spec((ATTN_WIDTH, ATTN_WIDTH))],
        out_specs=[_seg_spec(tm, ATTN_WIDTH), _seg_spec(tm, 4 * LANES), _seg_spec(tm, 4 * LANES),
                   _seg_spec(tm, CONV_WIDTH)],
        out_shape=[jax.ShapeDtypeStruct((n_seg, seg_len, ATTN_WIDTH), BF16),
                   jax.ShapeDtypeStruct((n_seg, seg_len, 4 * LANES), BF16),
                   jax.ShapeDtypeStruct((n_seg, seg_len, 4 * LANES), BF16),
                   jax.ShapeDtypeStruct((n_seg, seg_len, CONV_WIDTH), BF16)],
        compiler_params=_params(),
        name="in_proj",
    )(x, mods, w_in, gq, gk, cos, sin, bd)


def _kvproj_call(x, mods, w_kv, gk, cos, sin, bd_kv, *, tm, mod_row):
    n_seg, seg_len, _ = x.shape
    nt = seg_len // tm
    tab_spec = pl.BlockSpec((tm, LANES), lambda b, t: (t, 0))
    return pl.pallas_call(
        _kvproj_kernel,
        grid=(n_seg, nt),
        in_specs=[_seg_spec(tm, D_MODEL), _mod_spec(mod_row), _const_spec((D_MODEL, 2 * KV_WIDTH)),
                  _const_spec((1, KV_WIDTH)), tab_spec, tab_spec, _const_spec((KV_WIDTH, KV_WIDTH))],
        out_specs=[_seg_spec(tm, 4 * LANES), _seg_spec(tm, 4 * LANES)],
        out_shape=[jax.ShapeDtypeStruct((n_seg, seg_len, 4 * LANES), BF16),
                   jax.ShapeDtypeStruct((n_seg, seg_len, 4 * LANES), BF16)],
        compiler_params=_params(),
        name="kv_proj",
    )(x, mods, w_kv, gk, cos, sin, bd_kv)


def _attn_kernel(q_ref, *refs, n_src):
    o_ref = refs[2 * n_src]
    for j in range(ATTN_WIDTH // LANES):
        kvh = j // 2
        qp = q_ref[:, j * LANES:(j + 1) * LANES]
        out = None
        for par in range(2):
            c0 = (2 * kvh + par) * LANES
            scores = [lax.dot_general(qp, refs[2 * i][:, c0:c0 + LANES], NT_DIMS, preferred_element_type=F32)
                      for i in range(n_src)]
            m = scores[0].max(axis=-1, keepdims=True)
            for s in scores[1:]:
                m = jnp.maximum(m, s.max(axis=-1, keepdims=True))
            probs = [jnp.exp(s - m) for s in scores]
            denom = probs[0].sum(axis=-1, keepdims=True)
            for p in probs[1:]:
                denom = denom + p.sum(axis=-1, keepdims=True)
            o = jnp.dot(probs[0].astype(BF16), refs[1][:, c0:c0 + LANES], preferred_element_type=F32)
            for i in range(1, n_src):
                o = o + jnp.dot(probs[i].astype(BF16), refs[2 * i + 1][:, c0:c0 + LANES],
                                preferred_element_type=F32)
            o = o * (1.0 / denom)
            out = o if out is None else out + o
        o_ref[:, j * LANES:(j + 1) * LANES] = out.astype(BF16)


def _attn_call(q, sources, *, tq):
    n_seg, seg_len, _ = q.shape
    kv_specs = [pl.BlockSpec((None, s.shape[1], 4 * LANES), lambda b, t: (b, 0, 0)) for s in sources]
    return pl.pallas_call(
        functools.partial(_attn_kernel, n_src=len(sources) // 2),
        grid=(n_seg, seg_len // tq),
        in_specs=[_seg_spec(tq, ATTN_WIDTH)] + kv_specs,
        out_specs=_seg_spec(tq, ATTN_WIDTH),
        out_shape=jax.ShapeDtypeStruct((n_seg, seg_len, ATTN_WIDTH), BF16),
        compiler_params=_params(),
        name="attention",
    )(q, *sources)


def _conv_stage(a_ref, t, pad_ref, *, tm, nt):
    seg_len = tm * nt
    zero_halo = jnp.zeros((HALO, CONV_WIDTH), F32)
    if nt == 1:
        start = 0
        top, bot = zero_halo, zero_halo
    else:
        start = pl.multiple_of(t * tm, tm)
        top_start = pl.multiple_of(jnp.maximum(start - HALO, 0), HALO)
        bot_start = pl.multiple_of(jnp.minimum(start + tm, seg_len - HALO), HALO)
        top = jnp.where(t == 0, zero_halo, a_ref[pl.ds(top_start, HALO), :].astype(F32))
        bot = jnp.where(t == nt - 1, zero_halo, a_ref[pl.ds(bot_start, HALO), :].astype(F32))
    halo_tiles = HALO // SUBLANES
    row_tiles = tm // SUBLANES
    pad_ref[0:halo_tiles] = top.reshape(halo_tiles, SUBLANES, CONV_WIDTH)
    pad_ref[halo_tiles:halo_tiles + row_tiles] = (
        a_ref[pl.ds(start, tm), :].astype(F32).reshape(row_tiles, SUBLANES, CONV_WIDTH))
    pad_ref[halo_tiles + row_tiles:] = bot.reshape(halo_tiles, SUBLANES, CONV_WIDTH)


def _ordered_after(x, dep):
    assert CHAIN_DELAY_ROLLS >= 2
    bits = lax.bitcast_convert_type(dep, jnp.uint32)
    for _ in range(CHAIN_DELAY_ROLLS):
        bits = pltpu.roll(bits, 1, 1) >> 16
    return x + lax.bitcast_convert_type(bits, F32)


def _conv_column(c0, dww_ref, dwb_ref, pad_ref, shift_ref, y_ref):
    n_tiles = pad_ref.shape[0]
    row_tiles = y_ref.shape[0]
    sub = lax.broadcasted_iota(jnp.int32, (n_tiles - 1, SUBLANES, LANES), 1)
    for s in range(1, SUBLANES):
        rot = pltpu.roll(pad_ref[:, :, c0:c0 + LANES], SUBLANES - s, 1)
        shift_ref[s - 1] = jnp.where(sub < SUBLANES - s, rot[:-1], rot[1:])
    bias = jnp.broadcast_to(dwb_ref[:, c0:c0 + LANES], (SUBLANES, LANES))
    acc = None
    for k0 in range(0, row_tiles, CONV_TILES):
        init = bias if acc is None else _ordered_after(bias, acc[0])
        acc = jnp.broadcast_to(init, (CONV_TILES, SUBLANES, LANES))
        for j in range(CONV_KERNEL):
            m, s = divmod(HALO - CONV_HALF + j, SUBLANES)
            if s == 0:
                src = pad_ref[k0 + m:k0 + m + CONV_TILES, :, c0:c0 + LANES]
            else:
                src = shift_ref[s - 1, k0 + m:k0 + m + CONV_TILES]
            acc = acc + src * dww_ref[j, :, c0:c0 + LANES]
        y_ref[k0:k0 + CONV_TILES, :, c0:c0 + LANES] = acc


def _mixffn_kernel(a_ref, o_ref, x_ref, mod_ref, dww_ref, dwb_ref, cg_ref, cb_ref, wout_ref,
                   lg1_ref, lb1_ref, w1_ref, w3_ref, w2_ref, lg2_ref, lb2_ref, xo_ref,
                   conv_ref, pad_ref, shift_ref, y_ref, *, tm, nt):
    i = pl.program_id(0)
    n_steps = pl.num_programs(0)
    t = jnp.minimum(i, n_steps - 2) % nt

    @pl.when(i == 0)
    def _():
        conv_ref[1] = jnp.zeros((tm, CONV_WIDTH), BF16)

    assert len(FF_CHUNKS) == CONV_WIDTH // LANES
    mixed = jnp.concatenate([o_ref[...], conv_ref[(i + 1) % 2]], axis=-1)
    _conv_stage(a_ref, t, pad_ref, tm=tm, nt=nt)
    y = jnp.dot(mixed, wout_ref[...], preferred_element_type=F32)
    xm = _layer_norm(DN_ALPHA * x_ref[...] + mod_ref[2:3, :] * y, lg1_ref[...], lb1_ref[...])
    h = (xm * (1.0 + mod_ref[4:5, :]) + mod_ref[3:4, :]).astype(BF16)
    yf = None
    for k, (c0, width) in enumerate(FF_CHUNKS):
        _conv_column(k * LANES, dww_ref, dwb_ref, pad_ref, shift_ref, y_ref)
        a1 = jnp.dot(h, w1_ref[:, c0:c0 + width], preferred_element_type=F32)
        a3 = jnp.dot(h, w3_ref[:, c0:c0 + width], preferred_element_type=F32)
        hid = (a1 * jax.nn.sigmoid(a1) * a3).astype(BF16)
        part = jnp.dot(hid, w2_ref[c0:c0 + width, :], preferred_element_type=F32)
        yf = part if yf is None else yf + part
    yn = _layer_norm(y_ref[...].reshape(tm, CONV_WIDTH), cg_ref[...], cb_ref[...])
    conv_ref[i % 2] = (yn * jax.nn.sigmoid(yn)).astype(BF16)
    xo_ref[...] = _layer_norm(DN_ALPHA * xm + mod_ref[5:6, :] * yf, lg2_ref[...], lb2_ref[...])


def _mixffn_call(a, o, x, mods, dww8, dwb, cg, cb, w_out, lg1, lb1, w1, w3, w2, lg2, lb2, *, tm, mod_row):
    n_seg, seg_len, _ = x.shape
    nt = seg_len // tm
    n_total = n_seg * nt
    prev = lambda i: jnp.maximum(i - 1, 0)
    row_spec = lambda width: pl.BlockSpec((tm, width), lambda i: (prev(i), 0))
    if mod_row is None:
        mod_spec = pl.BlockSpec((None, 6, D_MODEL), lambda i: (prev(i) // nt, 0, 0))
    else:
        mod_spec = pl.BlockSpec((None, 6, D_MODEL), lambda i: (mod_row, 0, 0))
    n_win = (tm + 2 * HALO) // SUBLANES
    out = pl.pallas_call(
        functools.partial(_mixffn_kernel, tm=tm, nt=nt),
        grid=(n_total + 1,),
        in_specs=[pl.BlockSpec((None, seg_len, CONV_WIDTH), lambda i: (jnp.minimum(i, n_total - 1) // nt, 0, 0)),
                  row_spec(ATTN_WIDTH), row_spec(D_MODEL), mod_spec,
                  _const_spec((CONV_KERNEL, SUBLANES, CONV_WIDTH)), _const_spec((1, CONV_WIDTH)),
                  _const_spec((1, CONV_WIDTH)), _const_spec((1, CONV_WIDTH)),
                  _const_spec((D_MODEL, D_MODEL)), _const_spec((1, D_MODEL)), _const_spec((1, D_MODEL)),
                  _const_spec((D_MODEL, D_FF)), _const_spec((D_MODEL, D_FF)), _const_spec((D_FF, D_MODEL)),
                  _const_spec((1, D_MODEL)), _const_spec((1, D_MODEL))],
        out_specs=row_spec(D_MODEL),
        out_shape=jax.ShapeDtypeStruct((n_seg * seg_len, D_MODEL), F32),
        scratch_shapes=[pltpu.VMEM((2, tm, CONV_WIDTH), BF16),
                        pltpu.VMEM((n_win, SUBLANES, CONV_WIDTH), F32),
                        pltpu.VMEM((SUBLANES - 1, n_win - 1, SUBLANES, LANES), F32),
                        pltpu.VMEM((tm // SUBLANES, SUBLANES, CONV_WIDTH), F32)],
        compiler_params=_params(1),
        name="mix_ffn",
    )(a, o.reshape(n_seg * seg_len, ATTN_WIDTH), x.reshape(n_seg * seg_len, D_MODEL), mods, dww8, dwb, cg, cb,
      w_out, lg1, lb1, w1, w3, w2, lg2, lb2)
    return out.reshape(n_seg, seg_len, D_MODEL)


def _rope_tables(n):
    rows = n // GRID_W
    r, col = jnp.meshgrid(jnp.arange(rows), jnp.arange(GRID_W), indexing="ij")
    pos = jnp.stack([r.reshape(-1), col.reshape(-1)], axis=-1).astype(F32)
    n_freq = HEAD_DIM // 4
    freqs = ROPE_THETA ** (-jnp.arange(n_freq, dtype=F32) / n_freq)
    ang = pos[:, :, None] * freqs
    cos, sin = jnp.cos(ang), jnp.sin(ang)
    cos_head = jnp.concatenate([cos[:, 0], cos[:, 0], cos[:, 1], cos[:, 1]], axis=-1)
    sin_head = jnp.concatenate([-sin[:, 0], sin[:, 0], -sin[:, 1], sin[:, 1]], axis=-1)
    reps = LANES // HEAD_DIM
    return jnp.tile(cos_head, (1, reps)), jnp.tile(sin_head, (1, reps))


def kernel(x, c, ctx, c_ctx, w_ada, b_ada, w_in, q_norm_g, k_norm_g, dw_w, dw_b, conv_ln_g, conv_ln_b,
           w_out, ln1_g, ln1_b, w_ff1, w_ff3, w_ff2, ln2_g, ln2_b):
    n_batch, n_lat, _ = x.shape
    n_ctx = ctx.shape[1]
    assert n_batch + 1 <= MOD_ROWS
    ctx_row = n_batch
    lat_tm, ctx_tm = 512, n_ctx

    c_all = jnp.concatenate(
        [c, c_ctx[None, :], jnp.zeros((MOD_ROWS - n_batch - 1, D_MODEL), F32)], axis=0)
    mods = _ada_call(c_all, w_ada, b_ada.reshape(DEPTH, 1, 6 * D_MODEL))
    mods = mods.reshape(DEPTH, MOD_ROWS, 6, D_MODEL)

    cos_l, sin_l = _rope_tables(n_lat)
    cos_c, sin_c = jnp.ones((n_ctx, LANES), F32), jnp.zeros((n_ctx, LANES), F32)
    head_id = jnp.arange(ATTN_WIDTH) // HEAD_DIM
    bd = (head_id[:, None] == head_id[None, :]).astype(BF16)
    bd_kv = bd[:KV_WIDTH, :KV_WIDTH]

    w_in_b, w_out_b = w_in.astype(BF16), w_out.astype(BF16)
    w1_b, w3_b, w2_b = w_ff1.astype(BF16), w_ff3.astype(BF16), w_ff2.astype(BF16)
    gq = jnp.tile(q_norm_g, (1, N_Q_HEADS)).reshape(DEPTH, 1, ATTN_WIDTH)
    gk = jnp.tile(k_norm_g, (1, N_KV_HEADS)).reshape(DEPTH, 1, KV_WIDTH)
    dww8 = jnp.broadcast_to(dw_w.reshape(DEPTH, CONV_KERNEL, 1, CONV_WIDTH),
                            (DEPTH, CONV_KERNEL, SUBLANES, CONV_WIDTH))
    row = lambda p, l: p[l].reshape(1, -1)

    for l in range(DEPTH):
        last = l == DEPTH - 1
        q_l, kk_l, vv_l, a_l = _inproj_call(x, mods[l], w_in_b[l], gq[l], gk[l], cos_l, sin_l, bd,
                                            tm=lat_tm, mod_row=None)
        if last:
            kk_c, vv_c = _kvproj_call(ctx, mods[l], w_in_b[l][:, ATTN_WIDTH:ATTN_WIDTH + 2 * KV_WIDTH],
                                      gk[l], cos_c, sin_c, bd_kv, tm=ctx_tm, mod_row=ctx_row)
        else:
            q_c, kk_c, vv_c, a_c = _inproj_call(ctx, mods[l], w_in_b[l], gq[l], gk[l], cos_c, sin_c, bd,
                                                tm=ctx_tm, mod_row=ctx_row)
        tail_args = (dww8[l], row(dw_b, l), row(conv_ln_g, l), row(conv_ln_b, l), w_out_b[l],
                     row(ln1_g, l), row(ln1_b, l), w1_b[l], w3_b[l], w2_b[l], row(ln2_g, l), row(ln2_b, l))

        o_l = _attn_call(q_l, [kk_c, vv_c, kk_l, vv_l], tq=lat_tm)
        x = _mixffn_call(a_l, o_l, x, mods[l], *tail_args, tm=lat_tm, mod_row=None)

        if not last:
            o_c = _attn_call(q_c, [kk_c, vv_c], tq=ctx_tm)
            ctx = _mixffn_call(a_c, o_c, ctx, mods[l], *tail_args, tm=ctx_tm, mod_row=ctx_row)
    return x
```

```python
import functools

import jax
import jax.numpy as jnp
from jax import lax
from jax.experimental import pallas as pl
from jax.experimental.pallas import tpu as pltpu

F32 = jnp.float32
BF16 = jnp.bfloat16

D_MODEL = 1024
DEPTH = 4
GRID_W = 64
HEAD_DIM = 64
N_Q_HEADS = 8
N_KV_HEADS = 2
ATTN_WIDTH = N_Q_HEADS * HEAD_DIM
KV_WIDTH = N_KV_HEADS * HEAD_DIM
CONV_WIDTH = D_MODEL - ATTN_WIDTH
CONV_KERNEL = 31
CONV_HALF = CONV_KERNEL // 2
D_FF = 2816
ROPE_THETA = 10000.0
DN_ALPHA = (2 * DEPTH) ** 0.25
LN_EPS = 1e-6
RMS_EPS = 1e-6
ATTN_SCALE = HEAD_DIM ** -0.5
IN_WIDTH = ATTN_WIDTH + 2 * KV_WIDTH + 2 * CONV_WIDTH

LANES = 128
SUBLANES = 8
HALO = 16
ATTN_KEY_BLOCK = 2048
ATTN_UNIT_ROWS = 512
ATTN_TILE_ROWS = 1024
CONV_TILES = 4
CHAIN_DELAY_ROLLS = 4
MOD_ROWS = 16
VMEM_LIMIT = 52 * 1024 * 1024
FF_CHUNKS = ((0, 768), (768, 768), (1536, 768), (2304, 512))

NT_DIMS = (((1,), (1,)), ((), ()))


def _const_spec(shape):
    zeros = (0,) * len(shape)
    return pl.BlockSpec(shape, lambda *_: zeros, pipeline_mode=pl.Buffered(1))


def _layer_norm(z, g, b):
    mu = jnp.mean(z, axis=-1, keepdims=True)
    zc = z - mu
    var = jnp.mean(zc * zc, axis=-1, keepdims=True)
    return zc * lax.rsqrt(var + LN_EPS) * g + b


def _params(n_axes=2):
    return pltpu.CompilerParams(dimension_semantics=("arbitrary",) * n_axes, vmem_limit_bytes=VMEM_LIMIT)


def _ada_kernel(c_ref, w_ref, b_ref, o_ref):
    c = c_ref[...]
    s = (c * jax.nn.sigmoid(c)).astype(BF16)
    o_ref[...] = jnp.dot(s, w_ref[...].astype(BF16), preferred_element_type=F32) + b_ref[...]


def _ada_call(c_all, w_ada, b_ada):
    n_col = 6 * D_MODEL // D_MODEL
    return pl.pallas_call(
        _ada_kernel,
        grid=(DEPTH, n_col),
        in_specs=[
            pl.BlockSpec((MOD_ROWS, D_MODEL), lambda l, j: (0, 0)),
            pl.BlockSpec((None, D_MODEL, D_MODEL), lambda l, j: (l, 0, j)),
            pl.BlockSpec((None, 1, D_MODEL), lambda l, j: (l, 0, j)),
        ],
        out_specs=pl.BlockSpec((None, MOD_ROWS, D_MODEL), lambda l, j: (l, 0, j)),
        out_shape=jax.ShapeDtypeStruct((DEPTH, MOD_ROWS, 6 * D_MODEL), F32),
        compiler_params=_params(),
        name="ada_mod",
    )(c_all, w_ada, b_ada)


def _head_norm_rope(z, gain, bd, cos, sin):
    ssq = jnp.dot((z * z).astype(BF16), bd, preferred_element_type=F32)
    zn = z * lax.rsqrt(ssq * (1.0 / HEAD_DIM) + RMS_EPS) * gain
    outs = []
    for c0 in range(0, z.shape[1], LANES):
        zc = zn[:, c0:c0 + LANES]
        up = pltpu.roll(zc, LANES - 16, 1)
        dn = pltpu.roll(zc, 16, 1)
        lane = lax.broadcasted_iota(jnp.int32, zc.shape, 1)
        partner = jnp.where((lane & 16) == 0, up, dn)
        outs.append(zc * cos + partner * sin)
    return outs


def _kv_variants(z, fill):
    zs = pltpu.roll(z, HEAD_DIM, 1)
    lo = lax.broadcasted_iota(jnp.int32, z.shape, 1) < HEAD_DIM
    zero = jnp.full_like(z, fill)
    return jnp.concatenate(
        [jnp.where(lo, z, zero), jnp.where(lo, zero, zs), jnp.where(lo, zs, zero), jnp.where(lo, zero, z)],
        axis=-1).astype(BF16)


def _inproj_kernel(x_ref, mod_ref, w_ref, gq_ref, gk_ref, cos_ref, sin_ref, bd_ref,
                   q_ref, kk_ref, vv_ref, a_ref):
    h = (x_ref[...] * (1.0 + mod_ref[1:2, :]) + mod_ref[0:1, :]).astype(BF16)
    proj = jnp.dot(h, w_ref[...], preferred_element_type=F32)
    cos, sin = cos_ref[...], sin_ref[...]
    q = proj[:, :ATTN_WIDTH]
    k = proj[:, ATTN_WIDTH:ATTN_WIDTH + KV_WIDTH]
    v = proj[:, ATTN_WIDTH + KV_WIDTH:ATTN_WIDTH + 2 * KV_WIDTH]
    u = proj[:, ATTN_WIDTH + 2 * KV_WIDTH:ATTN_WIDTH + 2 * KV_WIDTH + CONV_WIDTH]
    g = proj[:, ATTN_WIDTH + 2 * KV_WIDTH + CONV_WIDTH:]
    qs = _head_norm_rope(q, gq_ref[...], bd_ref[...], cos, sin)
    q_ref[...] = (jnp.concatenate(qs, axis=-1) * ATTN_SCALE).astype(BF16)
    (kr,) = _head_norm_rope(k, gk_ref[...], bd_ref[:KV_WIDTH, :KV_WIDTH], cos, sin)
    kk_ref[...] = _kv_variants(kr, 0.0)
    vv_ref[...] = _kv_variants(v, 1.0)
    a_ref[...] = (u * jax.nn.sigmoid(g)).astype(BF16)


def _kvproj_kernel(x_ref, mod_ref, w_ref, gk_ref, cos_ref, sin_ref, bd_ref, kk_ref, vv_ref):
    h = (x_ref[...] * (1.0 + mod_ref[1:2, :]) + mod_ref[0:1, :]).astype(BF16)
    proj = jnp.dot(h, w_ref[...], preferred_element_type=F32)
    (kr,) = _head_norm_rope(proj[:, :KV_WIDTH], gk_ref[...], bd_ref[...], cos_ref[...], sin_ref[...])
    kk_ref[...] = _kv_variants(kr, 0.0)
    vv_ref[...] = _kv_variants(proj[:, KV_WIDTH:], 1.0)


def _seg_spec(tm, width):
    return pl.BlockSpec((None, tm, width), lambda b, t: (b, t, 0))


def _mod_spec(mod_row):
    if mod_row is None:
        return pl.BlockSpec((None, 6, D_MODEL), lambda b, t: (b, 0, 0))
    return pl.BlockSpec((None, 6, D_MODEL), lambda b, t: (mod_row, 0, 0))


def _inproj_call(x, mods, w_in, gq, gk, cos, sin, bd, *, tm, mod_row):
    n_seg, seg_len, _ = x.shape
    nt = seg_len // tm
    tab_spec = pl.BlockSpec((tm, LANES), lambda b, t: (t, 0))
    return pl.pallas_call(
        _inproj_kernel,
        grid=(n_seg, nt),
        in_specs=[_seg_spec(tm, D_MODEL), _mod_spec(mod_row), _const_spec((D_MODEL, IN_WIDTH)),
                  _const_spec((1, ATTN_WIDTH)), _const_spec((1, KV_WIDTH)), tab_spec, tab_spec,
                  _const_spec((ATTN_WIDTH, ATTN_WIDTH))],
        out_specs=[_seg_spec(tm, ATTN_WIDTH), _seg_spec(tm, 4 * LANES), _seg_spec(tm, 4 * LANES),
                   _seg_spec(tm, CONV_WIDTH)],
        out_shape=[jax.ShapeDtypeStruct((n_seg, seg_len, ATTN_WIDTH), BF16),
                   jax.ShapeDtypeStruct((n_seg, seg_len, 4 * LANES), BF16),
                   jax.ShapeDtypeStruct((n_seg, seg_len, 4 * LANES), BF16),
                   jax.ShapeDtypeStruct((n_seg, seg_len, CONV_WIDTH), BF16)],
        compiler_params=_params(),
        name="in_proj",
    )(x, mods, w_in, gq, gk, cos, sin, bd)


def _kvproj_call(x, mods, w_kv, gk, cos, sin, bd_kv, *, tm, mod_row):
    n_seg, seg_len, _ = x.shape
    nt = seg_len // tm
    tab_spec = pl.BlockSpec((tm, LANES), lambda b, t: (t, 0))
    return pl.pallas_call(
        _kvproj_kernel,
        grid=(n_seg, nt),
        in_specs=[_seg_spec(tm, D_MODEL), _mod_spec(mod_row), _const_spec((D_MODEL, 2 * KV_WIDTH)),
                  _const_spec((1, KV_WIDTH)), tab_spec, tab_spec, _const_spec((KV_WIDTH, KV_WIDTH))],
        out_specs=[_seg_spec(tm, 4 * LANES), _seg_spec(tm, 4 * LANES)],
        out_shape=[jax.ShapeDtypeStruct((n_seg, seg_len, 4 * LANES), BF16),
                   jax.ShapeDtypeStruct((n_seg, seg_len, 4 * LANES), BF16)],
        compiler_params=_params(),
        name="kv_proj",
    )(x, mods, w_kv, gk, cos, sin, bd_kv)


def _attn_kernel(q_ref, *refs, n_src):
    o_ref = refs[2 * n_src]
    blocks = []
    for i in range(n_src):
        n_keys = refs[2 * i].shape[0]
        step = min(n_keys, ATTN_KEY_BLOCK)
        blocks += [(refs[2 * i], refs[2 * i + 1], k0, step) for k0 in range(0, n_keys, step)]
    for j in range(ATTN_WIDTH // LANES):
        kvh = j // 2
        rows = min(q_ref.shape[0], ATTN_UNIT_ROWS)
        for r0 in range(0, q_ref.shape[0], rows):
            qp = q_ref[r0:r0 + rows, j * LANES:(j + 1) * LANES]
            halves = []
            for par in range(2):
                c0 = (2 * kvh + par) * LANES
                scores = [lax.dot_general(qp, k_ref[k0:k0 + nk, c0:c0 + LANES], NT_DIMS,
                                          preferred_element_type=F32) for k_ref, _, k0, nk in blocks]
                m = scores[0].max(axis=-1, keepdims=True)
                for s in scores[1:]:
                    m = jnp.maximum(m, s.max(axis=-1, keepdims=True))
                o = None
                for (_, v_ref, k0, nk), s in zip(blocks, scores):
                    part = jnp.dot(jnp.exp(s - m).astype(BF16), v_ref[k0:k0 + nk, c0:c0 + LANES],
                                   preferred_element_type=F32)
                    o = part if o is None else o + part
                halves.append(o * (1.0 / pltpu.roll(o, HEAD_DIM, 1)))
            lo = lax.broadcasted_iota(jnp.int32, halves[0].shape, 1) < HEAD_DIM
            o_ref[r0:r0 + rows, j * LANES:(j + 1) * LANES] = jnp.where(lo, halves[0], halves[1]).astype(BF16)


def _attn_call(q, sources, *, tq):
    n_seg, seg_len, _ = q.shape
    kv_specs = [pl.BlockSpec((None, s.shape[1], 4 * LANES), lambda b, t: (b, 0, 0)) for s in sources]
    return pl.pallas_call(
        functools.partial(_attn_kernel, n_src=len(sources) // 2),
        grid=(n_seg, seg_len // tq),
        in_specs=[_seg_spec(tq, ATTN_WIDTH)] + kv_specs,
        out_specs=_seg_spec(tq, ATTN_WIDTH),
        out_shape=jax.ShapeDtypeStruct((n_seg, seg_len, ATTN_WIDTH), BF16),
        compiler_params=_params(),
        name="attention",
    )(q, *sources)


def _conv_stage(a_ref, t, pad_ref, *, tm, nt):
    seg_len = tm * nt
    zero_halo = jnp.zeros((HALO, CONV_WIDTH), F32)
    if nt == 1:
        start = 0
        top, bot = zero_halo, zero_halo
    else:
        start = pl.multiple_of(t * tm, tm)
        top_start = pl.multiple_of(jnp.maximum(start - HALO, 0), HALO)
        bot_start = pl.multiple_of(jnp.minimum(start + tm, seg_len - HALO), HALO)
        top = jnp.where(t == 0, zero_halo, a_ref[pl.ds(top_start, HALO), :].astype(F32))
        bot = jnp.where(t == nt - 1, zero_halo, a_ref[pl.ds(bot_start, HALO), :].astype(F32))
    halo_tiles = HALO // SUBLANES
    row_tiles = tm // SUBLANES
    pad_ref[0:halo_tiles] = top.reshape(halo_tiles, SUBLANES, CONV_WIDTH)
    pad_ref[halo_tiles:halo_tiles + row_tiles] = (
        a_ref[pl.ds(start, tm), :].astype(F32).reshape(row_tiles, SUBLANES, CONV_WIDTH))
    pad_ref[halo_tiles + row_tiles:] = bot.reshape(halo_tiles, SUBLANES, CONV_WIDTH)


def _ordered_after(x, dep):
    assert CHAIN_DELAY_ROLLS >= 2
    bits = lax.bitcast_convert_type(dep, jnp.uint32)
    for _ in range(CHAIN_DELAY_ROLLS):
        bits = pltpu.roll(bits, 1, 1) >> 16
    return x + lax.bitcast_convert_type(bits, F32)


def _conv_column(c0, dww_ref, dwb_ref, pad_ref, shift_ref, y_ref):
    n_tiles = pad_ref.shape[0]
    row_tiles = y_ref.shape[0]
    sub = lax.broadcasted_iota(jnp.int32, (n_tiles - 1, SUBLANES, LANES), 1)
    for s in range(1, SUBLANES):
        rot = pltpu.roll(pad_ref[:, :, c0:c0 + LANES], SUBLANES - s, 1)
        shift_ref[s - 1] = jnp.where(sub < SUBLANES - s, rot[:-1], rot[1:])
    bias = jnp.broadcast_to(dwb_ref[:, c0:c0 + LANES], (SUBLANES, LANES))
    acc = None
    for k0 in range(0, row_tiles, CONV_TILES):
        init = bias if acc is None else _ordered_after(bias, acc[0])
        acc = jnp.broadcast_to(init, (CONV_TILES, SUBLANES, LANES))
        for j in range(CONV_KERNEL):
            m, s = divmod(HALO - CONV_HALF + j, SUBLANES)
            if s == 0:
                src = pad_ref[k0 + m:k0 + m + CONV_TILES, :, c0:c0 + LANES]
            else:
                src = shift_ref[s - 1, k0 + m:k0 + m + CONV_TILES]
            acc = acc + src * dww_ref[j, :, c0:c0 + LANES]
        y_ref[k0:k0 + CONV_TILES, :, c0:c0 + LANES] = acc


def _mixffn_kernel(a_ref, o_ref, x_ref, mod_ref, dww_ref, dwb_ref, cg_ref, cb_ref, wout_ref,
                   lg1_ref, lb1_ref, w1_ref, w3_ref, w2_ref, lg2_ref, lb2_ref, xo_ref,
                   conv_ref, pad_ref, shift_ref, y_ref, *, tm, nt):
    i = pl.program_id(0)
    n_steps = pl.num_programs(0)
    t = jnp.minimum(i, n_steps - 2) % nt

    @pl.when(i == 0)
    def _():
        conv_ref[1] = jnp.zeros((tm, CONV_WIDTH), BF16)

    assert len(FF_CHUNKS) == CONV_WIDTH // LANES
    mixed = jnp.concatenate([o_ref[...], conv_ref[(i + 1) % 2]], axis=-1)
    _conv_stage(a_ref, t, pad_ref, tm=tm, nt=nt)
    y = jnp.dot(mixed, wout_ref[...], preferred_element_type=F32)
    xm = _layer_norm(DN_ALPHA * x_ref[...] + mod_ref[2:3, :] * y, lg1_ref[...], lb1_ref[...])
    h = (xm * (1.0 + mod_ref[4:5, :]) + mod_ref[3:4, :]).astype(BF16)
    yf = None
    for k, (c0, width) in enumerate(FF_CHUNKS):
        _conv_column(k * LANES, dww_ref, dwb_ref, pad_ref, shift_ref, y_ref)
        a1 = jnp.dot(h, w1_ref[:, c0:c0 + width], preferred_element_type=F32)
        a3 = jnp.dot(h, w3_ref[:, c0:c0 + width], preferred_element_type=F32)
        hid = (a1 * jax.nn.sigmoid(a1) * a3).astype(BF16)
        part = jnp.dot(hid, w2_ref[c0:c0 + width, :], preferred_element_type=F32)
        yf = part if yf is None else yf + part
    yn = _layer_norm(y_ref[...].reshape(tm, CONV_WIDTH), cg_ref[...], cb_ref[...])
    conv_ref[i % 2] = (yn * jax.nn.sigmoid(yn)).astype(BF16)
    xo_ref[...] = _layer_norm(DN_ALPHA * xm + mod_ref[5:6, :] * yf, lg2_ref[...], lb2_ref[...])


def _mixffn_call(a, o, x, mods, dww8, dwb, cg, cb, w_out, lg1, lb1, w1, w3, w2, lg2, lb2, *, tm, mod_row):
    n_seg, seg_len, _ = x.shape
    nt = seg_len // tm
    n_total = n_seg * nt
    prev = lambda i: jnp.maximum(i - 1, 0)
    row_spec = lambda width: pl.BlockSpec((tm, width), lambda i: (prev(i), 0))
    if mod_row is None:
        mod_spec = pl.BlockSpec((None, 6, D_MODEL), lambda i: (prev(i) // nt, 0, 0))
    else:
        mod_spec = pl.BlockSpec((None, 6, D_MODEL), lambda i: (mod_row, 0, 0))
    n_win = (tm + 2 * HALO) // SUBLANES
    out = pl.pallas_call(
        functools.partial(_mixffn_kernel, tm=tm, nt=nt),
        grid=(n_total + 1,),
        in_specs=[pl.BlockSpec((None, seg_len, CONV_WIDTH), lambda i: (jnp.minimum(i, n_total - 1) // nt, 0, 0)),
                  row_spec(ATTN_WIDTH), row_spec(D_MODEL), mod_spec,
                  _const_spec((CONV_KERNEL, SUBLANES, CONV_WIDTH)), _const_spec((1, CONV_WIDTH)),
                  _const_spec((1, CONV_WIDTH)), _const_spec((1, CONV_WIDTH)),
                  _const_spec((D_MODEL, D_MODEL)), _const_spec((1, D_MODEL)), _const_spec((1, D_MODEL)),
                  _const_spec((D_MODEL, D_FF)), _const_spec((D_MODEL, D_FF)), _const_spec((D_FF, D_MODEL)),
                  _const_spec((1, D_MODEL)), _const_spec((1, D_MODEL))],
        out_specs=row_spec(D_MODEL),
        out_shape=jax.ShapeDtypeStruct((n_seg * seg_len, D_MODEL), F32),
        scratch_shapes=[pltpu.VMEM((2, tm, CONV_WIDTH), BF16),
                        pltpu.VMEM((n_win, SUBLANES, CONV_WIDTH), F32),
                        pltpu.VMEM((SUBLANES - 1, n_win - 1, SUBLANES, LANES), F32),
                        pltpu.VMEM((tm // SUBLANES, SUBLANES, CONV_WIDTH), F32)],
        compiler_params=_params(1),
        name="mix_ffn",
    )(a, o.reshape(n_seg * seg_len, ATTN_WIDTH), x.reshape(n_seg * seg_len, D_MODEL), mods, dww8, dwb, cg, cb,
      w_out, lg1, lb1, w1, w3, w2, lg2, lb2)
    return out.reshape(n_seg, seg_len, D_MODEL)


def _rope_tables(n):
    rows = n // GRID_W
    r, col = jnp.meshgrid(jnp.arange(rows), jnp.arange(GRID_W), indexing="ij")
    pos = jnp.stack([r.reshape(-1), col.reshape(-1)], axis=-1).astype(F32)
    n_freq = HEAD_DIM // 4
    freqs = ROPE_THETA ** (-jnp.arange(n_freq, dtype=F32) / n_freq)
    ang = pos[:, :, None] * freqs
    cos, sin = jnp.cos(ang), jnp.sin(ang)
    cos_head = jnp.concatenate([cos[:, 0], cos[:, 0], cos[:, 1], cos[:, 1]], axis=-1)
    sin_head = jnp.concatenate([-sin[:, 0], sin[:, 0], -sin[:, 1], sin[:, 1]], axis=-1)
    reps = LANES // HEAD_DIM
    return jnp.tile(cos_head, (1, reps)), jnp.tile(sin_head, (1, reps))


def kernel(x, c, ctx, c_ctx, w_ada, b_ada, w_in, q_norm_g, k_norm_g, dw_w, dw_b, conv_ln_g, conv_ln_b,
           w_out, ln1_g, ln1_b, w_ff1, w_ff3, w_ff2, ln2_g, ln2_b):
    n_batch, n_lat, _ = x.shape
    n_ctx = ctx.shape[1]
    assert n_batch + 1 <= MOD_ROWS
    ctx_row = n_batch
    lat_tm, ctx_tm = 512, n_ctx

    c_all = jnp.concatenate(
        [c, c_ctx[None, :], jnp.zeros((MOD_ROWS - n_batch - 1, D_MODEL), F32)], axis=0)
    mods = _ada_call(c_all, w_ada, b_ada.reshape(DEPTH, 1, 6 * D_MODEL))
    mods = mods.reshape(DEPTH, MOD_ROWS, 6, D_MODEL)

    cos_l, sin_l = _rope_tables(n_lat)
    cos_c, sin_c = jnp.ones((n_ctx, LANES), F32), jnp.zeros((n_ctx, LANES), F32)
    head_id = jnp.arange(ATTN_WIDTH) // HEAD_DIM
    bd = (head_id[:, None] == head_id[None, :]).astype(BF16)
    bd_kv = bd[:KV_WIDTH, :KV_WIDTH]

    w_in_b, w_out_b = w_in.astype(BF16), w_out.astype(BF16)
    w1_b, w3_b, w2_b = w_ff1.astype(BF16), w_ff3.astype(BF16), w_ff2.astype(BF16)
    gq = jnp.tile(q_norm_g, (1, N_Q_HEADS)).reshape(DEPTH, 1, ATTN_WIDTH)
    gk = jnp.tile(k_norm_g, (1, N_KV_HEADS)).reshape(DEPTH, 1, KV_WIDTH)
    dww8 = jnp.broadcast_to(dw_w.reshape(DEPTH, CONV_KERNEL, 1, CONV_WIDTH),
                            (DEPTH, CONV_KERNEL, SUBLANES, CONV_WIDTH))
    row = lambda p, l: p[l].reshape(1, -1)

    for l in range(DEPTH):
        last = l == DEPTH - 1
        q_l, kk_l, vv_l, a_l = _inproj_call(x, mods[l], w_in_b[l], gq[l], gk[l], cos_l, sin_l, bd,
                                            tm=lat_tm, mod_row=None)
        if last:
            kk_c, vv_c = _kvproj_call(ctx, mods[l], w_in_b[l][:, ATTN_WIDTH:ATTN_WIDTH + 2 * KV_WIDTH],
                                      gk[l], cos_c, sin_c, bd_kv, tm=ctx_tm, mod_row=ctx_row)
        else:
            q_c, kk_c, vv_c, a_c = _inproj_call(ctx, mods[l], w_in_b[l], gq[l], gk[l], cos_c, sin_c, bd,
                                                tm=ctx_tm, mod_row=ctx_row)
        tail_args = (dww8[l], row(dw_b, l), row(conv_ln_g, l), row(conv_ln_b, l), w_out_b[l],
                     row(ln1_g, l), row(ln1_b, l), w1_b[l], w3_b[l], w2_b[l], row(ln2_g, l), row(ln2_b, l))

        o_l = _attn_call(q_l, [kk_c, vv_c, kk_l, vv_l], tq=ATTN_TILE_ROWS)
        x = _mixffn_call(a_l, o_l, x, mods[l], *tail_args, tm=lat_tm, mod_row=None)

        if not last:
            o_c = _attn_call(q_c, [kk_c, vv_c], tq=ctx_tm)
            ctx = _mixffn_call(a_c, o_c, ctx, mods[l], *tail_args, tm=ctx_tm, mod_row=ctx_row)
    return x
```

```python
import functools

import jax
import jax.numpy as jnp
from jax import lax
from jax.experimental import pallas as pl
from jax.experimental.pallas import tpu as pltpu

F32 = jnp.float32
BF16 = jnp.bfloat16

D_MODEL = 1024
DEPTH = 4
GRID_W = 64
HEAD_DIM = 64
N_Q_HEADS = 8
N_KV_HEADS = 2
ATTN_WIDTH = N_Q_HEADS * HEAD_DIM
KV_WIDTH = N_KV_HEADS * HEAD_DIM
CONV_WIDTH = D_MODEL - ATTN_WIDTH
CONV_KERNEL = 31
CONV_HALF = CONV_KERNEL // 2
D_FF = 2816
ROPE_THETA = 10000.0
DN_ALPHA = (2 * DEPTH) ** 0.25
LN_EPS = 1e-6
RMS_EPS = 1e-6
ATTN_SCALE = HEAD_DIM ** -0.5
IN_WIDTH = ATTN_WIDTH + 2 * KV_WIDTH + 2 * CONV_WIDTH

LANES = 128
SUBLANES = 8
HALO = 16
ATTN_KEY_BLOCK = 2048
ATTN_UNIT_ROWS = 512
ATTN_TILE_ROWS = 1024
CONV_TILES = 4
CHAIN_DELAY_ROLLS = 4
MOD_ROWS = 16
VMEM_LIMIT = 52 * 1024 * 1024
FF_CHUNKS = ((0, 768), (768, 768), (1536, 768), (2304, 512))

NT_DIMS = (((1,), (1,)), ((), ()))


def _const_spec(shape):
    zeros = (0,) * len(shape)
    return pl.BlockSpec(shape, lambda *_: zeros, pipeline_mode=pl.Buffered(1))


def _layer_norm(z, g, b, eps=LN_EPS):
    mu = jnp.mean(z, axis=-1, keepdims=True)
    zc = z - mu
    var = jnp.mean(zc * zc, axis=-1, keepdims=True)
    return zc * lax.rsqrt(var + eps) * g + b


def _deepnorm(x, gate, y, g, b):
    return _layer_norm(x + (gate * (1.0 / DN_ALPHA)) * y, g, b, eps=LN_EPS / DN_ALPHA ** 2)


def _params(n_axes=2):
    return pltpu.CompilerParams(dimension_semantics=("arbitrary",) * n_axes, vmem_limit_bytes=VMEM_LIMIT)


def _ada_kernel(c_ref, w_ref, b_ref, o_ref):
    c = c_ref[...]
    s = (c * jax.nn.sigmoid(c)).astype(BF16)
    o_ref[...] = jnp.dot(s, w_ref[...].astype(BF16), preferred_element_type=F32) + b_ref[...]


def _ada_call(c_all, w_ada, b_ada):
    n_col = 6 * D_MODEL // D_MODEL
    return pl.pallas_call(
        _ada_kernel,
        grid=(DEPTH, n_col),
        in_specs=[
            pl.BlockSpec((MOD_ROWS, D_MODEL), lambda l, j: (0, 0)),
            pl.BlockSpec((None, D_MODEL, D_MODEL), lambda l, j: (l, 0, j)),
            pl.BlockSpec((None, 1, D_MODEL), lambda l, j: (l, 0, j)),
        ],
        out_specs=pl.BlockSpec((None, MOD_ROWS, D_MODEL), lambda l, j: (l, 0, j)),
        out_shape=jax.ShapeDtypeStruct((DEPTH, MOD_ROWS, 6 * D_MODEL), F32),
        compiler_params=_params(),
        name="ada_mod",
    )(c_all, w_ada, b_ada)


def _head_norm_rope(z, gain, bd, cos, sin):
    ssq = jnp.dot((z * z).astype(BF16), bd, preferred_element_type=F32)
    zn = z * lax.rsqrt(ssq * (1.0 / HEAD_DIM) + RMS_EPS) * gain
    outs = []
    for c0 in range(0, z.shape[1], LANES):
        zc = zn[:, c0:c0 + LANES]
        up = pltpu.roll(zc, LANES - 16, 1)
        dn = pltpu.roll(zc, 16, 1)
        lane = lax.broadcasted_iota(jnp.int32, zc.shape, 1)
        partner = jnp.where((lane & 16) == 0, up, dn)
        outs.append(zc * cos + partner * sin)
    return outs


def _kv_variants(z, fill):
    zs = pltpu.roll(z, HEAD_DIM, 1)
    lo = lax.broadcasted_iota(jnp.int32, z.shape, 1) < HEAD_DIM
    zero = jnp.full_like(z, fill)
    return jnp.concatenate(
        [jnp.where(lo, z, zero), jnp.where(lo, zero, zs), jnp.where(lo, zs, zero), jnp.where(lo, zero, z)],
        axis=-1).astype(BF16)


def _inproj_kernel(x_ref, mod_ref, w_ref, gq_ref, gk_ref, cos_ref, sin_ref, bd_ref,
                   q_ref, kk_ref, vv_ref, a_ref):
    h = (x_ref[...] * (1.0 + mod_ref[1:2, :]) + mod_ref[0:1, :]).astype(BF16)
    proj = jnp.dot(h, w_ref[...], preferred_element_type=F32)
    cos, sin = cos_ref[...], sin_ref[...]
    q = proj[:, :ATTN_WIDTH]
    k = proj[:, ATTN_WIDTH:ATTN_WIDTH + KV_WIDTH]
    v = proj[:, ATTN_WIDTH + KV_WIDTH:ATTN_WIDTH + 2 * KV_WIDTH]
    u = proj[:, ATTN_WIDTH + 2 * KV_WIDTH:ATTN_WIDTH + 2 * KV_WIDTH + CONV_WIDTH]
    g = proj[:, ATTN_WIDTH + 2 * KV_WIDTH + CONV_WIDTH:]
    qs = _head_norm_rope(q, gq_ref[...], bd_ref[...], cos, sin)
    q_ref[...] = (jnp.concatenate(qs, axis=-1) * ATTN_SCALE).astype(BF16)
    (kr,) = _head_norm_rope(k, gk_ref[...], bd_ref[:KV_WIDTH, :KV_WIDTH], cos, sin)
    kk_ref[...] = _kv_variants(kr, 0.0)
    vv_ref[...] = _kv_variants(v, 1.0)
    a_ref[...] = (u * jax.nn.sigmoid(g)).astype(BF16)


def _kvproj_kernel(x_ref, mod_ref, w_ref, gk_ref, cos_ref, sin_ref, bd_ref, kk_ref, vv_ref):
    h = (x_ref[...] * (1.0 + mod_ref[1:2, :]) + mod_ref[0:1, :]).astype(BF16)
    proj = jnp.dot(h, w_ref[...], preferred_element_type=F32)
    (kr,) = _head_norm_rope(proj[:, :KV_WIDTH], gk_ref[...], bd_ref[...], cos_ref[...], sin_ref[...])
    kk_ref[...] = _kv_variants(kr, 0.0)
    vv_ref[...] = _kv_variants(proj[:, KV_WIDTH:], 1.0)


def _seg_spec(tm, width):
    return pl.BlockSpec((None, tm, width), lambda b, t: (b, t, 0))


def _mod_spec(mod_row):
    if mod_row is None:
        return pl.BlockSpec((None, 6, D_MODEL), lambda b, t: (b, 0, 0))
    return pl.BlockSpec((None, 6, D_MODEL), lambda b, t: (mod_row, 0, 0))


def _inproj_call(x, mods, w_in, gq, gk, cos, sin, bd, *, tm, mod_row):
    n_seg, seg_len, _ = x.shape
    nt = seg_len // tm
    tab_spec = pl.BlockSpec((tm, LANES), lambda b, t: (t, 0))
    return pl.pallas_call(
        _inproj_kernel,
        grid=(n_seg, nt),
        in_specs=[_seg_spec(tm, D_MODEL), _mod_spec(mod_row), _const_spec((D_MODEL, IN_WIDTH)),
                  _const_spec((1, ATTN_WIDTH)), _const_spec((1, KV_WIDTH)), tab_spec, tab_spec,
                  _const_spec((ATTN_WIDTH, ATTN_WIDTH))],
        out_specs=[_seg_spec(tm, ATTN_WIDTH), _seg_spec(tm, 4 * LANES), _seg_spec(tm, 4 * LANES),
                   _seg_spec(tm, CONV_WIDTH)],
        out_shape=[jax.ShapeDtypeStruct((n_seg, seg_len, ATTN_WIDTH), BF16),
                   jax.ShapeDtypeStruct((n_seg, seg_len, 4 * LANES), BF16),
                   jax.ShapeDtypeStruct((n_seg, seg_len, 4 * LANES), BF16),
                   jax.ShapeDtypeStruct((n_seg, seg_len, CONV_WIDTH), BF16)],
        compiler_params=_params(),
        name="in_proj",
    )(x, mods, w_in, gq, gk, cos, sin, bd)


def _kvproj_call(x, mods, w_kv, gk, cos, sin, bd_kv, *, tm, mod_row):
    n_seg, seg_len, _ = x.shape
    nt = seg_len // tm
    tab_spec = pl.BlockSpec((tm, LANES), lambda b, t: (t, 0))
    return pl.pallas_call(
        _kvproj_kernel,
        grid=(n_seg, nt),
        in_specs=[_seg_spec(tm, D_MODEL), _mod_spec(mod_row), _const_spec((D_MODEL, 2 * KV_WIDTH)),
                  _const_spec((1, KV_WIDTH)), tab_spec, tab_spec, _const_spec((KV_WIDTH, KV_WIDTH))],
        out_specs=[_seg_spec(tm, 4 * LANES), _seg_spec(tm, 4 * LANES)],
        out_shape=[jax.ShapeDtypeStruct((n_seg, seg_len, 4 * LANES), BF16),
                   jax.ShapeDtypeStruct((n_seg, seg_len, 4 * LANES), BF16)],
        compiler_params=_params(),
        name="kv_proj",
    )(x, mods, w_kv, gk, cos, sin, bd_kv)


def _attn_kernel(q_ref, *refs, n_src):
    o_ref = refs[2 * n_src]
    blocks = []
    for i in range(n_src):
        n_keys = refs[2 * i].shape[0]
        step = min(n_keys, ATTN_KEY_BLOCK)
        blocks += [(refs[2 * i], refs[2 * i + 1], k0, step) for k0 in range(0, n_keys, step)]
    for j in range(ATTN_WIDTH // LANES):
        kvh = j // 2
        rows = min(q_ref.shape[0], ATTN_UNIT_ROWS)
        for r0 in range(0, q_ref.shape[0], rows):
            qp = q_ref[r0:r0 + rows, j * LANES:(j + 1) * LANES]
            halves = []
            for par in range(2):
                c0 = (2 * kvh + par) * LANES
                scores = [lax.dot_general(qp, k_ref[k0:k0 + nk, c0:c0 + LANES], NT_DIMS,
                                          preferred_element_type=F32) for k_ref, _, k0, nk in blocks]
                m = scores[0].max(axis=-1, keepdims=True)
                for s in scores[1:]:
                    m = jnp.maximum(m, s.max(axis=-1, keepdims=True))
                o = None
                for (_, v_ref, k0, nk), s in zip(blocks, scores):
                    part = jnp.dot(jnp.exp(s - m).astype(BF16), v_ref[k0:k0 + nk, c0:c0 + LANES],
                                   preferred_element_type=F32)
                    o = part if o is None else o + part
                halves.append(o * (1.0 / pltpu.roll(o, HEAD_DIM, 1)))
            lo = lax.broadcasted_iota(jnp.int32, halves[0].shape, 1) < HEAD_DIM
            o_ref[r0:r0 + rows, j * LANES:(j + 1) * LANES] = jnp.where(lo, halves[0], halves[1]).astype(BF16)


def _attn_call(q, sources, *, tq):
    n_seg, seg_len, _ = q.shape
    kv_specs = [pl.BlockSpec((None, s.shape[1], 4 * LANES), lambda b, t: (b, 0, 0)) for s in sources]
    return pl.pallas_call(
        functools.partial(_attn_kernel, n_src=len(sources) // 2),
        grid=(n_seg, seg_len // tq),
        in_specs=[_seg_spec(tq, ATTN_WIDTH)] + kv_specs,
        out_specs=_seg_spec(tq, ATTN_WIDTH),
        out_shape=jax.ShapeDtypeStruct((n_seg, seg_len, ATTN_WIDTH), BF16),
        compiler_params=_params(),
        name="attention",
    )(q, *sources)


def _conv_stage(a_ref, t, pad_ref, *, tm, nt):
    seg_len = tm * nt
    zero_halo = jnp.zeros((HALO, CONV_WIDTH), F32)
    if nt == 1:
        start = 0
        top, bot = zero_halo, zero_halo
    else:
        start = pl.multiple_of(t * tm, tm)
        top_start = pl.multiple_of(jnp.maximum(start - HALO, 0), HALO)
        bot_start = pl.multiple_of(jnp.minimum(start + tm, seg_len - HALO), HALO)
        top = jnp.where(t == 0, zero_halo, a_ref[pl.ds(top_start, HALO), :].astype(F32))
        bot = jnp.where(t == nt - 1, zero_halo, a_ref[pl.ds(bot_start, HALO), :].astype(F32))
    halo_tiles = HALO // SUBLANES
    row_tiles = tm // SUBLANES
    pad_ref[0:halo_tiles] = top.reshape(halo_tiles, SUBLANES, CONV_WIDTH)
    pad_ref[halo_tiles:halo_tiles + row_tiles] = (
        a_ref[pl.ds(start, tm), :].astype(F32).reshape(row_tiles, SUBLANES, CONV_WIDTH))
    pad_ref[halo_tiles + row_tiles:] = bot.reshape(halo_tiles, SUBLANES, CONV_WIDTH)


def _ordered_after(x, dep):
    assert CHAIN_DELAY_ROLLS >= 2
    bits = lax.bitcast_convert_type(dep, jnp.uint32)
    for _ in range(CHAIN_DELAY_ROLLS):
        bits = pltpu.roll(bits, 1, 1) >> 16
    return x + lax.bitcast_convert_type(bits, F32)


def _conv_column(c0, dww_ref, dwb_ref, pad_ref, shift_ref, y_ref, *, chained):
    n_tiles = pad_ref.shape[0]
    row_tiles = y_ref.shape[0]
    sub = lax.broadcasted_iota(jnp.int32, (n_tiles - 1, SUBLANES, LANES), 1)
    for s in range(1, SUBLANES):
        rot = pltpu.roll(pad_ref[:, :, c0:c0 + LANES], SUBLANES - s, 1)
        shift_ref[s - 1] = jnp.where(sub < SUBLANES - s, rot[:-1], rot[1:])
    bias = jnp.broadcast_to(dwb_ref[:, c0:c0 + LANES], (SUBLANES, LANES))
    acc = None
    for k0 in range(0, row_tiles, CONV_TILES):
        init = _ordered_after(bias, acc[0]) if chained and acc is not None else bias
        acc = jnp.broadcast_to(init, (CONV_TILES, SUBLANES, LANES))
        for j in range(CONV_KERNEL):
            m, s = divmod(HALO - CONV_HALF + j, SUBLANES)
            if s == 0:
                src = pad_ref[k0 + m:k0 + m + CONV_TILES, :, c0:c0 + LANES]
            else:
                src = shift_ref[s - 1, k0 + m:k0 + m + CONV_TILES]
            acc = acc + src * dww_ref[j, :, c0:c0 + LANES]
        y_ref[k0:k0 + CONV_TILES, :, c0:c0 + LANES] = acc


def _mixffn_kernel(a_ref, o_ref, x_ref, mod_ref, dww_ref, dwb_ref, cg_ref, cb_ref, wout_ref,
                   lg1_ref, lb1_ref, w1_ref, w3_ref, w2_ref, lg2_ref, lb2_ref, xo_ref,
                   conv_ref, pad_ref, shift_ref, y_ref, *, tm, nt):
    i = pl.program_id(0)
    n_steps = pl.num_programs(0)
    t = jnp.minimum(i, n_steps - 2) % nt

    def conv_branch(chained):
        _conv_stage(a_ref, t, pad_ref, tm=tm, nt=nt)
        for c0 in range(0, CONV_WIDTH, LANES):
            _conv_column(c0, dww_ref, dwb_ref, pad_ref, shift_ref, y_ref, chained=chained)
        yn = _layer_norm(y_ref[...].reshape(tm, CONV_WIDTH), cg_ref[...], cb_ref[...])
        conv_ref[i % 2] = (yn * jax.nn.sigmoid(yn)).astype(BF16)

    @pl.when(i == 0)
    def _():
        conv_branch(chained=False)

    @pl.when(i > 0)
    def _():
        mixed = jnp.concatenate([o_ref[...], conv_ref[(i + 1) % 2]], axis=-1)
        y = jnp.dot(mixed, wout_ref[...], preferred_element_type=F32)
        xm = _deepnorm(x_ref[...], mod_ref[2:3, :], y, lg1_ref[...], lb1_ref[...])
        h = (xm * (1.0 + mod_ref[4:5, :]) + mod_ref[3:4, :]).astype(BF16)
        yf = None
        for c0, width in FF_CHUNKS:
            a1 = jnp.dot(h, w1_ref[:, c0:c0 + width], preferred_element_type=F32)
            a3 = jnp.dot(h, w3_ref[:, c0:c0 + width], preferred_element_type=F32)
            hid = (a1 * jax.nn.sigmoid(a1) * a3).astype(BF16)
            part = jnp.dot(hid, w2_ref[c0:c0 + width, :], preferred_element_type=F32)
            yf = part if yf is None else yf + part
        xo_ref[...] = _deepnorm(xm, mod_ref[5:6, :], yf, lg2_ref[...], lb2_ref[...])
        conv_branch(chained=True)


def _mixffn_call(a, o, x, mods, dww8, dwb, cg, cb, w_out, lg1, lb1, w1, w3, w2, lg2, lb2, *, tm, mod_row):
    n_seg, seg_len, _ = x.shape
    nt = seg_len // tm
    n_total = n_seg * nt
    prev = lambda i: jnp.maximum(i - 1, 0)
    row_spec = lambda width: pl.BlockSpec((tm, width), lambda i: (prev(i), 0))
    if mod_row is None:
        mod_spec = pl.BlockSpec((None, 6, D_MODEL), lambda i: (prev(i) // nt, 0, 0))
    else:
        mod_spec = pl.BlockSpec((None, 6, D_MODEL), lambda i: (mod_row, 0, 0))
    n_win = (tm + 2 * HALO) // SUBLANES
    out = pl.pallas_call(
        functools.partial(_mixffn_kernel, tm=tm, nt=nt),
        grid=(n_total + 1,),
        in_specs=[pl.BlockSpec((None, seg_len, CONV_WIDTH), lambda i: (jnp.minimum(i, n_total - 1) // nt, 0, 0)),
                  row_spec(ATTN_WIDTH), row_spec(D_MODEL), mod_spec,
                  _const_spec((CONV_KERNEL, SUBLANES, CONV_WIDTH)), _const_spec((1, CONV_WIDTH)),
                  _const_spec((1, CONV_WIDTH)), _const_spec((1, CONV_WIDTH)),
                  _const_spec((D_MODEL, D_MODEL)), _const_spec((1, D_MODEL)), _const_spec((1, D_MODEL)),
                  _const_spec((D_MODEL, D_FF)), _const_spec((D_MODEL, D_FF)), _const_spec((D_FF, D_MODEL)),
                  _const_spec((1, D_MODEL)), _const_spec((1, D_MODEL))],
        out_specs=row_spec(D_MODEL),
        out_shape=jax.ShapeDtypeStruct((n_seg * seg_len, D_MODEL), F32),
        scratch_shapes=[pltpu.VMEM((2, tm, CONV_WIDTH), BF16),
                        pltpu.VMEM((n_win, SUBLANES, CONV_WIDTH), F32),
                        pltpu.VMEM((SUBLANES - 1, n_win - 1, SUBLANES, LANES), F32),
                        pltpu.VMEM((tm // SUBLANES, SUBLANES, CONV_WIDTH), F32)],
        compiler_params=_params(1),
        name="mix_ffn",
    )(a, o.reshape(n_seg * seg_len, ATTN_WIDTH), x.reshape(n_seg * seg_len, D_MODEL), mods, dww8, dwb, cg, cb,
      w_out, lg1, lb1, w1, w3, w2, lg2, lb2)
    return out.reshape(n_seg, seg_len, D_MODEL)


def _rope_tables(n):
    rows = n // GRID_W
    r, col = jnp.meshgrid(jnp.arange(rows), jnp.arange(GRID_W), indexing="ij")
    pos = jnp.stack([r.reshape(-1), col.reshape(-1)], axis=-1).astype(F32)
    n_freq = HEAD_DIM // 4
    freqs = ROPE_THETA ** (-jnp.arange(n_freq, dtype=F32) / n_freq)
    ang = pos[:, :, None] * freqs
    cos, sin = jnp.cos(ang), jnp.sin(ang)
    cos_head = jnp.concatenate([cos[:, 0], cos[:, 0], cos[:, 1], cos[:, 1]], axis=-1)
    sin_head = jnp.concatenate([-sin[:, 0], sin[:, 0], -sin[:, 1], sin[:, 1]], axis=-1)
    reps = LANES // HEAD_DIM
    return jnp.tile(cos_head, (1, reps)), jnp.tile(sin_head, (1, reps))


def kernel(x, c, ctx, c_ctx, w_ada, b_ada, w_in, q_norm_g, k_norm_g, dw_w, dw_b, conv_ln_g, conv_ln_b,
           w_out, ln1_g, ln1_b, w_ff1, w_ff3, w_ff2, ln2_g, ln2_b):
    n_batch, n_lat, _ = x.shape
    n_ctx = ctx.shape[1]
    assert n_batch + 1 <= MOD_ROWS
    ctx_row = n_batch
    lat_tm, ctx_tm = 512, n_ctx

    c_all = jnp.concatenate(
        [c, c_ctx[None, :], jnp.zeros((MOD_ROWS - n_batch - 1, D_MODEL), F32)], axis=0)
    mods = _ada_call(c_all, w_ada, b_ada.reshape(DEPTH, 1, 6 * D_MODEL))
    mods = mods.reshape(DEPTH, MOD_ROWS, 6, D_MODEL)

    cos_l, sin_l = _rope_tables(n_lat)
    cos_c, sin_c = jnp.ones((n_ctx, LANES), F32), jnp.zeros((n_ctx, LANES), F32)
    head_id = jnp.arange(ATTN_WIDTH) // HEAD_DIM
    bd = (head_id[:, None] == head_id[None, :]).astype(BF16)
    bd_kv = bd[:KV_WIDTH, :KV_WIDTH]

    w_in_b, w_out_b = w_in.astype(BF16), w_out.astype(BF16)
    w1_b, w3_b, w2_b = w_ff1.astype(BF16), w_ff3.astype(BF16), w_ff2.astype(BF16)
    gq = jnp.tile(q_norm_g, (1, N_Q_HEADS)).reshape(DEPTH, 1, ATTN_WIDTH)
    gk = jnp.tile(k_norm_g, (1, N_KV_HEADS)).reshape(DEPTH, 1, KV_WIDTH)
    dww8 = jnp.broadcast_to(dw_w.reshape(DEPTH, CONV_KERNEL, 1, CONV_WIDTH),
                            (DEPTH, CONV_KERNEL, SUBLANES, CONV_WIDTH))
    row = lambda p, l: p[l].reshape(1, -1)

    for l in range(DEPTH):
        last = l == DEPTH - 1
        q_l, kk_l, vv_l, a_l = _inproj_call(x, mods[l], w_in_b[l], gq[l], gk[l], cos_l, sin_l, bd,
                                            tm=lat_tm, mod_row=None)
        if last:
            kk_c, vv_c = _kvproj_call(ctx, mods[l], w_in_b[l][:, ATTN_WIDTH:ATTN_WIDTH + 2 * KV_WIDTH],
                                      gk[l], cos_c, sin_c, bd_kv, tm=ctx_tm, mod_row=ctx_row)
        else:
            q_c, kk_c, vv_c, a_c = _inproj_call(ctx, mods[l], w_in_b[l], gq[l], gk[l], cos_c, sin_c, bd,
                                                tm=ctx_tm, mod_row=ctx_row)
        tail_args = (dww8[l], row(dw_b, l), row(conv_ln_g, l), row(conv_ln_b, l), w_out_b[l],
                     row(ln1_g, l), row(ln1_b, l), w1_b[l], w3_b[l], w2_b[l], row(ln2_g, l), row(ln2_b, l))

        o_l = _attn_call(q_l, [kk_c, vv_c, kk_l, vv_l], tq=ATTN_TILE_ROWS)
        x = _mixffn_call(a_l, o_l, x, mods[l], *tail_args, tm=lat_tm, mod_row=None)

        if not last:
            o_c = _attn_call(q_c, [kk_c, vv_c], tq=ctx_tm)
            ctx = _mixffn_call(a_c, o_c, ctx, mods[l], *tail_args, tm=ctx_tm, mod_row=ctx_row)
    return x
```

```python
import functools

import jax
import jax.numpy as jnp
from jax import lax
from jax.experimental import pallas as pl
from jax.experimental.pallas import tpu as pltpu

F32 = jnp.float32
BF16 = jnp.bfloat16

D_MODEL = 1024
DEPTH = 4
GRID_W = 64
HEAD_DIM = 64
N_Q_HEADS = 8
N_KV_HEADS = 2
ATTN_WIDTH = N_Q_HEADS * HEAD_DIM
KV_WIDTH = N_KV_HEADS * HEAD_DIM
CONV_WIDTH = D_MODEL - ATTN_WIDTH
CONV_KERNEL = 31
CONV_HALF = CONV_KERNEL // 2
D_FF = 2816
ROPE_THETA = 10000.0
DN_ALPHA = (2 * DEPTH) ** 0.25
LN_EPS = 1e-6
RMS_EPS = 1e-6
ATTN_SCALE = HEAD_DIM ** -0.5
IN_WIDTH = ATTN_WIDTH + 2 * KV_WIDTH + 2 * CONV_WIDTH

LANES = 128
SUBLANES = 8
HALO = 16
ATTN_KEY_BLOCK = 2048
ATTN_UNIT_ROWS = 512
ATTN_TILE_ROWS = 2048
CONV_TILES = 4
CHAIN_DELAY_ROLLS = 4
MOD_ROWS = 16
VMEM_LIMIT = 52 * 1024 * 1024
FF_CHUNKS = ((0, 768), (768, 768), (1536, 768), (2304, 512))

NT_DIMS = (((1,), (1,)), ((), ()))


def _const_spec(shape):
    zeros = (0,) * len(shape)
    return pl.BlockSpec(shape, lambda *_: zeros, pipeline_mode=pl.Buffered(1))


def _layer_norm(z, g, b, eps=LN_EPS):
    mu = jnp.mean(z, axis=-1, keepdims=True)
    zc = z - mu
    var = jnp.mean(zc * zc, axis=-1, keepdims=True)
    return zc * lax.rsqrt(var + eps) * g + b


def _deepnorm(x, gate, y, g, b):
    return _layer_norm(x + (gate * (1.0 / DN_ALPHA)) * y, g, b, eps=LN_EPS / DN_ALPHA ** 2)


def _params(n_axes=2):
    return pltpu.CompilerParams(dimension_semantics=("arbitrary",) * n_axes, vmem_limit_bytes=VMEM_LIMIT)


def _ada_kernel(c_ref, w_ref, b_ref, o_ref):
    c = c_ref[...]
    s = (c * jax.nn.sigmoid(c)).astype(BF16)
    o_ref[...] = jnp.dot(s, w_ref[...].astype(BF16), preferred_element_type=F32) + b_ref[...]


def _ada_call(c_all, w_ada, b_ada):
    n_col = 6 * D_MODEL // D_MODEL
    return pl.pallas_call(
        _ada_kernel,
        grid=(DEPTH, n_col),
        in_specs=[
            pl.BlockSpec((MOD_ROWS, D_MODEL), lambda l, j: (0, 0)),
            pl.BlockSpec((None, D_MODEL, D_MODEL), lambda l, j: (l, 0, j)),
            pl.BlockSpec((None, 1, D_MODEL), lambda l, j: (l, 0, j)),
        ],
        out_specs=pl.BlockSpec((None, MOD_ROWS, D_MODEL), lambda l, j: (l, 0, j)),
        out_shape=jax.ShapeDtypeStruct((DEPTH, MOD_ROWS, 6 * D_MODEL), F32),
        compiler_params=_params(),
        name="ada_mod",
    )(c_all, w_ada, b_ada)


def _head_norm_rope(z, gain, bd, cos, sin):
    ssq = jnp.dot((z * z).astype(BF16), bd, preferred_element_type=F32)
    zn = z * lax.rsqrt(ssq * (1.0 / HEAD_DIM) + RMS_EPS) * gain
    outs = []
    for c0 in range(0, z.shape[1], LANES):
        zc = zn[:, c0:c0 + LANES]
        up = pltpu.roll(zc, LANES - 16, 1)
        dn = pltpu.roll(zc, 16, 1)
        lane = lax.broadcasted_iota(jnp.int32, zc.shape, 1)
        partner = jnp.where((lane & 16) == 0, up, dn)
        outs.append(zc * cos + partner * sin)
    return outs


def _kv_variants(z, fill):
    zs = pltpu.roll(z, HEAD_DIM, 1)
    lo = lax.broadcasted_iota(jnp.int32, z.shape, 1) < HEAD_DIM
    zero = jnp.full_like(z, fill)
    return jnp.concatenate(
        [jnp.where(lo, z, zero), jnp.where(lo, zero, zs), jnp.where(lo, zs, zero), jnp.where(lo, zero, z)],
        axis=-1).astype(BF16)


def _inproj_kernel(x_ref, mod_ref, w_ref, gq_ref, gk_ref, cos_ref, sin_ref, bd_ref,
                   q_ref, kk_ref, vv_ref, a_ref):
    h = (x_ref[...] * (1.0 + mod_ref[1:2, :]) + mod_ref[0:1, :]).astype(BF16)
    proj = jnp.dot(h, w_ref[...], preferred_element_type=F32)
    cos, sin = cos_ref[...], sin_ref[...]
    q = proj[:, :ATTN_WIDTH]
    k = proj[:, ATTN_WIDTH:ATTN_WIDTH + KV_WIDTH]
    v = proj[:, ATTN_WIDTH + KV_WIDTH:ATTN_WIDTH + 2 * KV_WIDTH]
    u = proj[:, ATTN_WIDTH + 2 * KV_WIDTH:ATTN_WIDTH + 2 * KV_WIDTH + CONV_WIDTH]
    g = proj[:, ATTN_WIDTH + 2 * KV_WIDTH + CONV_WIDTH:]
    qs = _head_norm_rope(q, gq_ref[...], bd_ref[...], cos, sin)
    q_ref[...] = (jnp.concatenate(qs, axis=-1) * ATTN_SCALE).astype(BF16)
    (kr,) = _head_norm_rope(k, gk_ref[...], bd_ref[:KV_WIDTH, :KV_WIDTH], cos, sin)
    kk_ref[...] = _kv_variants(kr, 0.0)
    vv_ref[...] = _kv_variants(v, 1.0)
    a_ref[...] = (u * jax.nn.sigmoid(g)).astype(BF16)


def _kvproj_kernel(x_ref, mod_ref, w_ref, gk_ref, cos_ref, sin_ref, bd_ref, kk_ref, vv_ref):
    h = (x_ref[...] * (1.0 + mod_ref[1:2, :]) + mod_ref[0:1, :]).astype(BF16)
    proj = jnp.dot(h, w_ref[...], preferred_element_type=F32)
    (kr,) = _head_norm_rope(proj[:, :KV_WIDTH], gk_ref[...], bd_ref[...], cos_ref[...], sin_ref[...])
    kk_ref[...] = _kv_variants(kr, 0.0)
    vv_ref[...] = _kv_variants(proj[:, KV_WIDTH:], 1.0)


def _seg_spec(tm, width):
    return pl.BlockSpec((None, tm, width), lambda b, t: (b, t, 0))


def _mod_spec(mod_row):
    if mod_row is None:
        return pl.BlockSpec((None, 6, D_MODEL), lambda b, t: (b, 0, 0))
    return pl.BlockSpec((None, 6, D_MODEL), lambda b, t: (mod_row, 0, 0))


def _inproj_call(x, mods, w_in, gq, gk, cos, sin, bd, *, tm, mod_row):
    n_seg, seg_len, _ = x.shape
    nt = seg_len // tm
    tab_spec = pl.BlockSpec((tm, LANES), lambda b, t: (t, 0))
    return pl.pallas_call(
        _inproj_kernel,
        grid=(n_seg, nt),
        in_specs=[_seg_spec(tm, D_MODEL), _mod_spec(mod_row), _const_spec((D_MODEL, IN_WIDTH)),
                  _const_spec((1, ATTN_WIDTH)), _const_spec((1, KV_WIDTH)), tab_spec, tab_spec,
                  _const_spec((ATTN_WIDTH, ATTN_WIDTH))],
        out_specs=[_seg_spec(tm, ATTN_WIDTH), _seg_spec(tm, 4 * LANES), _seg_spec(tm, 4 * LANES),
                   _seg_spec(tm, CONV_WIDTH)],
        out_shape=[jax.ShapeDtypeStruct((n_seg, seg_len, ATTN_WIDTH), BF16),
                   jax.ShapeDtypeStruct((n_seg, seg_len, 4 * LANES), BF16),
                   jax.ShapeDtypeStruct((n_seg, seg_len, 4 * LANES), BF16),
                   jax.ShapeDtypeStruct((n_seg, seg_len, CONV_WIDTH), BF16)],
        compiler_params=_params(),
        name="in_proj",
    )(x, mods, w_in, gq, gk, cos, sin, bd)


def _kvproj_call(x, mods, w_kv, gk, cos, sin, bd_kv, *, tm, mod_row):
    n_seg, seg_len, _ = x.shape
    nt = seg_len // tm
    tab_spec = pl.BlockSpec((tm, LANES), lambda b, t: (t, 0))
    return pl.pallas_call(
        _kvproj_kernel,
        grid=(n_seg, nt),
        in_specs=[_seg_spec(tm, D_MODEL), _mod_spec(mod_row), _const_spec((D_MODEL, 2 * KV_WIDTH)),
                  _const_spec((1, KV_WIDTH)), tab_spec, tab_spec, _const_spec((KV_WIDTH, KV_WIDTH))],
        out_specs=[_seg_spec(tm, 4 * LANES), _seg_spec(tm, 4 * LANES)],
        out_shape=[jax.ShapeDtypeStruct((n_seg, seg_len, 4 * LANES), BF16),
                   jax.ShapeDtypeStruct((n_seg, seg_len, 4 * LANES), BF16)],
        compiler_params=_params(),
        name="kv_proj",
    )(x, mods, w_kv, gk, cos, sin, bd_kv)


def _attn_kernel(q_ref, *refs, n_src):
    o_ref = refs[2 * n_src]
    blocks = []
    for i in range(n_src):
        n_keys = refs[2 * i].shape[0]
        step = min(n_keys, ATTN_KEY_BLOCK)
        blocks += [(refs[2 * i], refs[2 * i + 1], k0, step) for k0 in range(0, n_keys, step)]
    for j in range(ATTN_WIDTH // LANES):
        kvh = j // 2
        rows = min(q_ref.shape[0], ATTN_UNIT_ROWS)
        for r0 in range(0, q_ref.shape[0], rows):
            qp = q_ref[r0:r0 + rows, j * LANES:(j + 1) * LANES]
            halves = []
            for par in range(2):
                c0 = (2 * kvh + par) * LANES
                scores = [lax.dot_general(qp, k_ref[k0:k0 + nk, c0:c0 + LANES], NT_DIMS,
                                          preferred_element_type=F32) for k_ref, _, k0, nk in blocks]
                m = scores[0].max(axis=-1, keepdims=True)
                for s in scores[1:]:
                    m = jnp.maximum(m, s.max(axis=-1, keepdims=True))
                o = None
                for (_, v_ref, k0, nk), s in zip(blocks, scores):
                    part = jnp.dot(jnp.exp(s - m).astype(BF16), v_ref[k0:k0 + nk, c0:c0 + LANES],
                                   preferred_element_type=F32)
                    o = part if o is None else o + part
                halves.append(o * (1.0 / pltpu.roll(o, HEAD_DIM, 1)))
            lo = lax.broadcasted_iota(jnp.int32, halves[0].shape, 1) < HEAD_DIM
            o_ref[r0:r0 + rows, j * LANES:(j + 1) * LANES] = jnp.where(lo, halves[0], halves[1]).astype(BF16)


def _attn_call(q, sources, *, tq):
    n_seg, seg_len, _ = q.shape
    kv_specs = [pl.BlockSpec((None, s.shape[1], 4 * LANES), lambda b, t: (b, 0, 0)) for s in sources]
    return pl.pallas_call(
        functools.partial(_attn_kernel, n_src=len(sources) // 2),
        grid=(n_seg, seg_len // tq),
        in_specs=[_seg_spec(tq, ATTN_WIDTH)] + kv_specs,
        out_specs=_seg_spec(tq, ATTN_WIDTH),
        out_shape=jax.ShapeDtypeStruct((n_seg, seg_len, ATTN_WIDTH), BF16),
        compiler_params=_params(),
        name="attention",
    )(q, *sources)


def _conv_stage(a_ref, t, pad_ref, *, tm, nt):
    seg_len = tm * nt
    zero_halo = jnp.zeros((HALO, CONV_WIDTH), F32)
    if nt == 1:
        start = 0
        top, bot = zero_halo, zero_halo
    else:
        start = pl.multiple_of(t * tm, tm)
        top_start = pl.multiple_of(jnp.maximum(start - HALO, 0), HALO)
        bot_start = pl.multiple_of(jnp.minimum(start + tm, seg_len - HALO), HALO)
        top = jnp.where(t == 0, zero_halo, a_ref[pl.ds(top_start, HALO), :].astype(F32))
        bot = jnp.where(t == nt - 1, zero_halo, a_ref[pl.ds(bot_start, HALO), :].astype(F32))
    halo_tiles = HALO // SUBLANES
    row_tiles = tm // SUBLANES
    pad_ref[0:halo_tiles] = top.reshape(halo_tiles, SUBLANES, CONV_WIDTH)
    pad_ref[halo_tiles:halo_tiles + row_tiles] = (
        a_ref[pl.ds(start, tm), :].astype(F32).reshape(row_tiles, SUBLANES, CONV_WIDTH))
    pad_ref[halo_tiles + row_tiles:] = bot.reshape(halo_tiles, SUBLANES, CONV_WIDTH)


def _ordered_after(x, dep):
    assert CHAIN_DELAY_ROLLS >= 2
    bits = lax.bitcast_convert_type(dep, jnp.uint32)
    for _ in range(CHAIN_DELAY_ROLLS):
        bits = pltpu.roll(bits, 1, 1) >> 16
    return x + lax.bitcast_convert_type(bits, F32)


def _conv_column(c0, dww_ref, dwb_ref, pad_ref, shift_ref, y_ref, *, chained):
    n_tiles = pad_ref.shape[0]
    row_tiles = y_ref.shape[0]
    sub = lax.broadcasted_iota(jnp.int32, (n_tiles - 1, SUBLANES, LANES), 1)
    for s in range(1, SUBLANES):
        rot = pltpu.roll(pad_ref[:, :, c0:c0 + LANES], SUBLANES - s, 1)
        shift_ref[s - 1] = jnp.where(sub < SUBLANES - s, rot[:-1], rot[1:])
    bias = jnp.broadcast_to(dwb_ref[:, c0:c0 + LANES], (SUBLANES, LANES))
    acc = None
    for k0 in range(0, row_tiles, CONV_TILES):
        init = _ordered_after(bias, acc[0]) if chained and acc is not None else bias
        acc = jnp.broadcast_to(init, (CONV_TILES, SUBLANES, LANES))
        for j in range(CONV_KERNEL):
            m, s = divmod(HALO - CONV_HALF + j, SUBLANES)
            if s == 0:
                src = pad_ref[k0 + m:k0 + m + CONV_TILES, :, c0:c0 + LANES]
            else:
                src = shift_ref[s - 1, k0 + m:k0 + m + CONV_TILES]
            acc = acc + src * dww_ref[j, :, c0:c0 + LANES]
        y_ref[k0:k0 + CONV_TILES, :, c0:c0 + LANES] = acc


def _mixffn_kernel(a_ref, o_ref, x_ref, mod_ref, dww_ref, dwb_ref, cg_ref, cb_ref, wout_ref,
                   lg1_ref, lb1_ref, w1_ref, w3_ref, w2_ref, lg2_ref, lb2_ref, xo_ref,
                   conv_ref, pad_ref, shift_ref, y_ref, *, tm, nt):
    i = pl.program_id(0)
    n_steps = pl.num_programs(0)
    t = jnp.minimum(i, n_steps - 2) % nt

    def conv_branch(chained):
        _conv_stage(a_ref, t, pad_ref, tm=tm, nt=nt)
        for c0 in range(0, CONV_WIDTH, LANES):
            _conv_column(c0, dww_ref, dwb_ref, pad_ref, shift_ref, y_ref, chained=chained)
        yn = _layer_norm(y_ref[...].reshape(tm, CONV_WIDTH), cg_ref[...], cb_ref[...])
        conv_ref[i % 2] = (yn * jax.nn.sigmoid(yn)).astype(BF16)

    @pl.when(i == 0)
    def _():
        conv_branch(chained=False)

    @pl.when(i > 0)
    def _():
        mixed = jnp.concatenate([o_ref[...], conv_ref[(i + 1) % 2]], axis=-1)
        y = jnp.dot(mixed, wout_ref[...], preferred_element_type=F32)
        xm = _deepnorm(x_ref[...], mod_ref[2:3, :], y, lg1_ref[...], lb1_ref[...])
        h = (xm * (1.0 + mod_ref[4:5, :]) + mod_ref[3:4, :]).astype(BF16)
        yf = None
        for c0, width in FF_CHUNKS:
            a1 = jnp.dot(h, w1_ref[:, c0:c0 + width], preferred_element_type=F32)
            a3 = jnp.dot(h, w3_ref[:, c0:c0 + width], preferred_element_type=F32)
            hid = (a1 * jax.nn.sigmoid(a1) * a3).astype(BF16)
            part = jnp.dot(hid, w2_ref[c0:c0 + width, :], preferred_element_type=F32)
            yf = part if yf is None else yf + part
        xo_ref[...] = _deepnorm(xm, mod_ref[5:6, :], yf, lg2_ref[...], lb2_ref[...])
        conv_branch(chained=True)


def _mixffn_call(a, o, x, mods, dww8, dwb, cg, cb, w_out, lg1, lb1, w1, w3, w2, lg2, lb2, *, tm, mod_row):
    n_seg, seg_len, _ = x.shape
    nt = seg_len // tm
    n_total = n_seg * nt
    prev = lambda i: jnp.maximum(i - 1, 0)
    row_spec = lambda width: pl.BlockSpec((tm, width), lambda i: (prev(i), 0))
    if mod_row is None:
        mod_spec = pl.BlockSpec((None, 6, D_MODEL), lambda i: (prev(i) // nt, 0, 0))
    else:
        mod_spec = pl.BlockSpec((None, 6, D_MODEL), lambda i: (mod_row, 0, 0))
    n_win = (tm + 2 * HALO) // SUBLANES
    out = pl.pallas_call(
        functools.partial(_mixffn_kernel, tm=tm, nt=nt),
        grid=(n_total + 1,),
        in_specs=[pl.BlockSpec((None, seg_len, CONV_WIDTH), lambda i: (jnp.minimum(i, n_total - 1) // nt, 0, 0)),
                  row_spec(ATTN_WIDTH), row_spec(D_MODEL), mod_spec,
                  _const_spec((CONV_KERNEL, SUBLANES, CONV_WIDTH)), _const_spec((1, CONV_WIDTH)),
                  _const_spec((1, CONV_WIDTH)), _const_spec((1, CONV_WIDTH)),
                  _const_spec((D_MODEL, D_MODEL)), _const_spec((1, D_MODEL)), _const_spec((1, D_MODEL)),
                  _const_spec((D_MODEL, D_FF)), _const_spec((D_MODEL, D_FF)), _const_spec((D_FF, D_MODEL)),
                  _const_spec((1, D_MODEL)), _const_spec((1, D_MODEL))],
        out_specs=row_spec(D_MODEL),
        out_shape=jax.ShapeDtypeStruct((n_seg * seg_len, D_MODEL), F32),
        scratch_shapes=[pltpu.VMEM((2, tm, CONV_WIDTH), BF16),
                        pltpu.VMEM((n_win, SUBLANES, CONV_WIDTH), F32),
                        pltpu.VMEM((SUBLANES - 1, n_win - 1, SUBLANES, LANES), F32),
                        pltpu.VMEM((tm // SUBLANES, SUBLANES, CONV_WIDTH), F32)],
        compiler_params=_params(1),
        name="mix_ffn",
    )(a, o.reshape(n_seg * seg_len, ATTN_WIDTH), x.reshape(n_seg * seg_len, D_MODEL), mods, dww8, dwb, cg, cb,
      w_out, lg1, lb1, w1, w3, w2, lg2, lb2)
    return out.reshape(n_seg, seg_len, D_MODEL)


def _rope_tables(n):
    rows = n // GRID_W
    r, col = jnp.meshgrid(jnp.arange(rows), jnp.arange(GRID_W), indexing="ij")
    pos = jnp.stack([r.reshape(-1), col.reshape(-1)], axis=-1).astype(F32)
    n_freq = HEAD_DIM // 4
    freqs = ROPE_THETA ** (-jnp.arange(n_freq, dtype=F32) / n_freq)
    ang = pos[:, :, None] * freqs
    cos, sin = jnp.cos(ang), jnp.sin(ang)
    cos_head = jnp.concatenate([cos[:, 0], cos[:, 0], cos[:, 1], cos[:, 1]], axis=-1)
    sin_head = jnp.concatenate([-sin[:, 0], sin[:, 0], -sin[:, 1], sin[:, 1]], axis=-1)
    reps = LANES // HEAD_DIM
    return jnp.tile(cos_head, (1, reps)), jnp.tile(sin_head, (1, reps))


def kernel(x, c, ctx, c_ctx, w_ada, b_ada, w_in, q_norm_g, k_norm_g, dw_w, dw_b, conv_ln_g, conv_ln_b,
           w_out, ln1_g, ln1_b, w_ff1, w_ff3, w_ff2, ln2_g, ln2_b):
    n_batch, n_lat, _ = x.shape
    n_ctx = ctx.shape[1]
    assert n_batch + 1 <= MOD_ROWS
    ctx_row = n_batch
    lat_tm, ctx_tm = 512, n_ctx

    c_all = jnp.concatenate(
        [c, c_ctx[None, :], jnp.zeros((MOD_ROWS - n_batch - 1, D_MODEL), F32)], axis=0)
    mods = _ada_call(c_all, w_ada, b_ada.reshape(DEPTH, 1, 6 * D_MODEL))
    mods = mods.reshape(DEPTH, MOD_ROWS, 6, D_MODEL)

    cos_l, sin_l = _rope_tables(n_lat)
    cos_c, sin_c = jnp.ones((n_ctx, LANES), F32), jnp.zeros((n_ctx, LANES), F32)
    head_id = jnp.arange(ATTN_WIDTH) // HEAD_DIM
    bd = (head_id[:, None] == head_id[None, :]).astype(BF16)
    bd_kv = bd[:KV_WIDTH, :KV_WIDTH]

    w_in_b, w_out_b = w_in.astype(BF16), w_out.astype(BF16)
    w1_b, w3_b, w2_b = w_ff1.astype(BF16), w_ff3.astype(BF16), w_ff2.astype(BF16)
    gq = jnp.tile(q_norm_g, (1, N_Q_HEADS)).reshape(DEPTH, 1, ATTN_WIDTH)
    gk = jnp.tile(k_norm_g, (1, N_KV_HEADS)).reshape(DEPTH, 1, KV_WIDTH)
    dww8 = jnp.broadcast_to(dw_w.reshape(DEPTH, CONV_KERNEL, 1, CONV_WIDTH),
                            (DEPTH, CONV_KERNEL, SUBLANES, CONV_WIDTH))
    row = lambda p, l: p[l].reshape(1, -1)

    for l in range(DEPTH):
        last = l == DEPTH - 1
        q_l, kk_l, vv_l, a_l = _inproj_call(x, mods[l], w_in_b[l], gq[l], gk[l], cos_l, sin_l, bd,
                                            tm=lat_tm, mod_row=None)
        if last:
            kk_c, vv_c = _kvproj_call(ctx, mods[l], w_in_b[l][:, ATTN_WIDTH:ATTN_WIDTH + 2 * KV_WIDTH],
                                      gk[l], cos_c, sin_c, bd_kv, tm=ctx_tm, mod_row=ctx_row)
        else:
            q_c, kk_c, vv_c, a_c = _inproj_call(ctx, mods[l], w_in_b[l], gq[l], gk[l], cos_c, sin_c, bd,
                                                tm=ctx_tm, mod_row=ctx_row)
        tail_args = (dww8[l], row(dw_b, l), row(conv_ln_g, l), row(conv_ln_b, l), w_out_b[l],
                     row(ln1_g, l), row(ln1_b, l), w1_b[l], w3_b[l], w2_b[l], row(ln2_g, l), row(ln2_b, l))

        o_l = _attn_call(q_l, [kk_c, vv_c, kk_l, vv_l], tq=ATTN_TILE_ROWS)
        x = _mixffn_call(a_l, o_l, x, mods[l], *tail_args, tm=lat_tm, mod_row=None)

        if not last:
            o_c = _attn_call(q_c, [kk_c, vv_c], tq=ctx_tm)
            ctx = _mixffn_call(a_c, o_c, ctx, mods[l], *tail_args, tm=ctx_tm, mod_row=ctx_row)
    return x
```

```python
import functools

import jax
import jax.numpy as jnp
from jax import lax
from jax.experimental import pallas as pl
from jax.experimental.pallas import tpu as pltpu

F32 = jnp.float32
BF16 = jnp.bfloat16

D_MODEL = 1024
DEPTH = 4
GRID_W = 64
HEAD_DIM = 64
N_Q_HEADS = 8
N_KV_HEADS = 2
ATTN_WIDTH = N_Q_HEADS * HEAD_DIM
KV_WIDTH = N_KV_HEADS * HEAD_DIM
CONV_WIDTH = D_MODEL - ATTN_WIDTH
CONV_KERNEL = 31
CONV_HALF = CONV_KERNEL // 2
D_FF = 2816
ROPE_THETA = 10000.0
DN_ALPHA = (2 * DEPTH) ** 0.25
LN_EPS = 1e-6
RMS_EPS = 1e-6
ATTN_SCALE = HEAD_DIM ** -0.5
IN_WIDTH = ATTN_WIDTH + 2 * KV_WIDTH + 2 * CONV_WIDTH

LANES = 128
SUBLANES = 8
HALO = 16
ATTN_KEY_BLOCK = 2048
ATTN_UNIT_ROWS = 512
ATTN_TILE_ROWS = 1024
CONV_TILES = 4
CHAIN_DELAY_ROLLS = 4
MOD_ROWS = 16
VMEM_LIMIT = 52 * 1024 * 1024

NT_DIMS = (((1,), (1,)), ((), ()))


def _const_spec(shape):
    zeros = (0,) * len(shape)
    return pl.BlockSpec(shape, lambda *_: zeros, pipeline_mode=pl.Buffered(1))


def _layer_norm(z, g, b, eps=LN_EPS):
    mu = jnp.mean(z, axis=-1, keepdims=True)
    zc = z - mu
    var = jnp.mean(zc * zc, axis=-1, keepdims=True)
    return zc * lax.rsqrt(var + eps) * g + b


def _deepnorm(x, gate, y, g, b):
    return _layer_norm(x + (gate * (1.0 / DN_ALPHA)) * y, g, b, eps=LN_EPS / DN_ALPHA ** 2)


def _params(n_axes=2):
    return pltpu.CompilerParams(dimension_semantics=("arbitrary",) * n_axes, vmem_limit_bytes=VMEM_LIMIT)


def _ada_kernel(c_ref, w_ref, b_ref, o_ref):
    c = c_ref[...]
    s = (c * jax.nn.sigmoid(c)).astype(BF16)
    o_ref[...] = jnp.dot(s, w_ref[...].astype(BF16), preferred_element_type=F32) + b_ref[...]


def _ada_call(c_all, w_ada, b_ada):
    n_col = 6 * D_MODEL // D_MODEL
    return pl.pallas_call(
        _ada_kernel,
        grid=(DEPTH, n_col),
        in_specs=[
            pl.BlockSpec((MOD_ROWS, D_MODEL), lambda l, j: (0, 0)),
            pl.BlockSpec((None, D_MODEL, D_MODEL), lambda l, j: (l, 0, j)),
            pl.BlockSpec((None, 1, D_MODEL), lambda l, j: (l, 0, j)),
        ],
        out_specs=pl.BlockSpec((None, MOD_ROWS, D_MODEL), lambda l, j: (l, 0, j)),
        out_shape=jax.ShapeDtypeStruct((DEPTH, MOD_ROWS, 6 * D_MODEL), F32),
        compiler_params=_params(),
        name="ada_mod",
    )(c_all, w_ada, b_ada)


def _head_norm_rope(z, gain, bd, cos, sin):
    ssq = jnp.dot((z * z).astype(BF16), bd, preferred_element_type=F32)
    zn = z * lax.rsqrt(ssq * (1.0 / HEAD_DIM) + RMS_EPS) * gain
    outs = []
    for c0 in range(0, z.shape[1], LANES):
        zc = zn[:, c0:c0 + LANES]
        up = pltpu.roll(zc, LANES - 16, 1)
        dn = pltpu.roll(zc, 16, 1)
        lane = lax.broadcasted_iota(jnp.int32, zc.shape, 1)
        partner = jnp.where((lane & 16) == 0, up, dn)
        outs.append(zc * cos + partner * sin)
    return outs


def _kv_variants(z, fill):
    zs = pltpu.roll(z, HEAD_DIM, 1)
    lo = lax.broadcasted_iota(jnp.int32, z.shape, 1) < HEAD_DIM
    zero = jnp.full_like(z, fill)
    return jnp.concatenate(
        [jnp.where(lo, z, zero), jnp.where(lo, zero, zs), jnp.where(lo, zs, zero), jnp.where(lo, zero, z)],
        axis=-1).astype(BF16)


def _inproj_kernel(x_ref, mod_ref, w_ref, gq_ref, gk_ref, cos_ref, sin_ref, bd_ref,
                   q_ref, kk_ref, vv_ref, a_ref):
    h = (x_ref[...] * (1.0 + mod_ref[1:2, :]) + mod_ref[0:1, :]).astype(BF16)
    proj = jnp.dot(h, w_ref[...], preferred_element_type=F32)
    cos, sin = cos_ref[...], sin_ref[...]
    q = proj[:, :ATTN_WIDTH]
    k = proj[:, ATTN_WIDTH:ATTN_WIDTH + KV_WIDTH]
    v = proj[:, ATTN_WIDTH + KV_WIDTH:ATTN_WIDTH + 2 * KV_WIDTH]
    u = proj[:, ATTN_WIDTH + 2 * KV_WIDTH:ATTN_WIDTH + 2 * KV_WIDTH + CONV_WIDTH]
    g = proj[:, ATTN_WIDTH + 2 * KV_WIDTH + CONV_WIDTH:]
    qs = _head_norm_rope(q, gq_ref[...], bd_ref[...], cos, sin)
    q_ref[...] = (jnp.concatenate(qs, axis=-1) * ATTN_SCALE).astype(BF16)
    (kr,) = _head_norm_rope(k, gk_ref[...], bd_ref[:KV_WIDTH, :KV_WIDTH], cos, sin)
    kk_ref[...] = _kv_variants(kr, 0.0)
    vv_ref[...] = _kv_variants(v, 1.0)
    a_ref[...] = (u * jax.nn.sigmoid(g)).astype(BF16)


def _kvproj_kernel(x_ref, mod_ref, w_ref, gk_ref, cos_ref, sin_ref, bd_ref, kk_ref, vv_ref):
    h = (x_ref[...] * (1.0 + mod_ref[1:2, :]) + mod_ref[0:1, :]).astype(BF16)
    proj = jnp.dot(h, w_ref[...], preferred_element_type=F32)
    (kr,) = _head_norm_rope(proj[:, :KV_WIDTH], gk_ref[...], bd_ref[...], cos_ref[...], sin_ref[...])
    kk_ref[...] = _kv_variants(kr, 0.0)
    vv_ref[...] = _kv_variants(proj[:, KV_WIDTH:], 1.0)


def _seg_spec(tm, width):
    return pl.BlockSpec((None, tm, width), lambda b, t: (b, t, 0))


def _mod_spec(mod_row):
    if mod_row is None:
        return pl.BlockSpec((None, 6, D_MODEL), lambda b, t: (b, 0, 0))
    return pl.BlockSpec((None, 6, D_MODEL), lambda b, t: (mod_row, 0, 0))


def _inproj_call(x, mods, w_in, gq, gk, cos, sin, bd, *, tm, mod_row):
    n_seg, seg_len, _ = x.shape
    nt = seg_len // tm
    tab_spec = pl.BlockSpec((tm, LANES), lambda b, t: (t, 0))
    return pl.pallas_call(
        _inproj_kernel,
        grid=(n_seg, nt),
        in_specs=[_seg_spec(tm, D_MODEL), _mod_spec(mod_row), _const_spec((D_MODEL, IN_WIDTH)),
                  _const_spec((1, ATTN_WIDTH)), _const_spec((1, KV_WIDTH)), tab_spec, tab_spec,
                  _const_spec((ATTN_WIDTH, ATTN_WIDTH))],
        out_specs=[_seg_spec(tm, ATTN_WIDTH), _seg_spec(tm, 4 * LANES), _seg_spec(tm, 4 * LANES),
                   _seg_spec(tm, CONV_WIDTH)],
        out_shape=[jax.ShapeDtypeStruct((n_seg, seg_len, ATTN_WIDTH), BF16),
                   jax.ShapeDtypeStruct((n_seg, seg_len, 4 * LANES), BF16),
                   jax.ShapeDtypeStruct((n_seg, seg_len, 4 * LANES), BF16),
                   jax.ShapeDtypeStruct((n_seg, seg_len, CONV_WIDTH), BF16)],
        compiler_params=_params(),
        name="in_proj",
    )(x, mods, w_in, gq, gk, cos, sin, bd)


def _kvproj_call(x, mods, w_kv, gk, cos, sin, bd_kv, *, tm, mod_row):
    n_seg, seg_len, _ = x.shape
    nt = seg_len // tm
    tab_spec = pl.BlockSpec((tm, LANES), lambda b, t: (t, 0))
    return pl.pallas_call(
        _kvproj_kernel,
        grid=(n_seg, nt),
        in_specs=[_seg_spec(tm, D_MODEL), _mod_spec(mod_row), _const_spec((D_MODEL, 2 * KV_WIDTH)),
                  _const_spec((1, KV_WIDTH)), tab_spec, tab_spec, _const_spec((KV_WIDTH, KV_WIDTH))],
        out_specs=[_seg_spec(tm, 4 * LANES), _seg_spec(tm, 4 * LANES)],
        out_shape=[jax.ShapeDtypeStruct((n_seg, seg_len, 4 * LANES), BF16),
                   jax.ShapeDtypeStruct((n_seg, seg_len, 4 * LANES), BF16)],
        compiler_params=_params(),
        name="kv_proj",
    )(x, mods, w_kv, gk, cos, sin, bd_kv)


def _attn_kernel(q_ref, *refs, n_src):
    o_ref = refs[2 * n_src]
    blocks = []
    for i in range(n_src):
        n_keys = refs[2 * i].shape[0]
        step = min(n_keys, ATTN_KEY_BLOCK)
        blocks += [(refs[2 * i], refs[2 * i + 1], k0, step) for k0 in range(0, n_keys, step)]
    for j in range(ATTN_WIDTH // LANES):
        kvh = j // 2
        rows = min(q_ref.shape[0], ATTN_UNIT_ROWS)
        for r0 in range(0, q_ref.shape[0], rows):
            qp = q_ref[r0:r0 + rows, j * LANES:(j + 1) * LANES]
            halves = []
            for par in range(2):
                c0 = (2 * kvh + par) * LANES
                scores = [lax.dot_general(qp, k_ref[k0:k0 + nk, c0:c0 + LANES], NT_DIMS,
                                          preferred_element_type=F32) for k_ref, _, k0, nk in blocks]
                m = scores[0].max(axis=-1, keepdims=True)
                for s in scores[1:]:
                    m = jnp.maximum(m, s.max(axis=-1, keepdims=True))
                o = None
                for (_, v_ref, k0, nk), s in zip(blocks, scores):
                    part = jnp.dot(jnp.exp(s - m).astype(BF16), v_ref[k0:k0 + nk, c0:c0 + LANES],
                                   preferred_element_type=F32)
                    o = part if o is None else o + part
                halves.append(o * (1.0 / pltpu.roll(o, HEAD_DIM, 1)))
            lo = lax.broadcasted_iota(jnp.int32, halves[0].shape, 1) < HEAD_DIM
            o_ref[r0:r0 + rows, j * LANES:(j + 1) * LANES] = jnp.where(lo, halves[0], halves[1]).astype(BF16)


def _attn_call(q, sources, *, tq):
    n_seg, seg_len, _ = q.shape
    kv_specs = [pl.BlockSpec((None, s.shape[1], 4 * LANES), lambda b, t: (b, 0, 0)) for s in sources]
    return pl.pallas_call(
        functools.partial(_attn_kernel, n_src=len(sources) // 2),
        grid=(n_seg, seg_len // tq),
        in_specs=[_seg_spec(tq, ATTN_WIDTH)] + kv_specs,
        out_specs=_seg_spec(tq, ATTN_WIDTH),
        out_shape=jax.ShapeDtypeStruct((n_seg, seg_len, ATTN_WIDTH), BF16),
        compiler_params=_params(),
        name="attention",
    )(q, *sources)


def _conv_stage(a_ref, t, pad_ref, *, tm, nt):
    seg_len = tm * nt
    zero_halo = jnp.zeros((HALO, CONV_WIDTH), F32)
    if nt == 1:
        start = 0
        top, bot = zero_halo, zero_halo
    else:
        start = pl.multiple_of(t * tm, tm)
        top_start = pl.multiple_of(jnp.maximum(start - HALO, 0), HALO)
        bot_start = pl.multiple_of(jnp.minimum(start + tm, seg_len - HALO), HALO)
        top = jnp.where(t == 0, zero_halo, a_ref[pl.ds(top_start, HALO), :].astype(F32))
        bot = jnp.where(t == nt - 1, zero_halo, a_ref[pl.ds(bot_start, HALO), :].astype(F32))
    halo_tiles = HALO // SUBLANES
    row_tiles = tm // SUBLANES
    pad_ref[0:halo_tiles] = top.reshape(halo_tiles, SUBLANES, CONV_WIDTH)
    pad_ref[halo_tiles:halo_tiles + row_tiles] = (
        a_ref[pl.ds(start, tm), :].astype(F32).reshape(row_tiles, SUBLANES, CONV_WIDTH))
    pad_ref[halo_tiles + row_tiles:] = bot.reshape(halo_tiles, SUBLANES, CONV_WIDTH)


def _ordered_after(x, dep):
    assert CHAIN_DELAY_ROLLS >= 2
    bits = lax.bitcast_convert_type(dep, jnp.uint32)
    for _ in range(CHAIN_DELAY_ROLLS):
        bits = pltpu.roll(bits, 1, 1) >> 16
    return x + lax.bitcast_convert_type(bits, F32)


def _conv_column(c0, dww_ref, dwb_ref, pad_ref, shift_ref, y_ref):
    n_tiles = pad_ref.shape[0]
    row_tiles = y_ref.shape[0]
    sub = lax.broadcasted_iota(jnp.int32, (n_tiles - 1, SUBLANES, LANES), 1)
    for s in range(1, SUBLANES):
        rot = pltpu.roll(pad_ref[:, :, c0:c0 + LANES], SUBLANES - s, 1)
        shift_ref[s - 1] = jnp.where(sub < SUBLANES - s, rot[:-1], rot[1:])
    bias = jnp.broadcast_to(dwb_ref[:, c0:c0 + LANES], (SUBLANES, LANES))
    acc = None
    for k0 in range(0, row_tiles, CONV_TILES):
        init = bias if acc is None else _ordered_after(bias, acc[0])
        acc = jnp.broadcast_to(init, (CONV_TILES, SUBLANES, LANES))
        for j in range(CONV_KERNEL):
            m, s = divmod(HALO - CONV_HALF + j, SUBLANES)
            if s == 0:
                src = pad_ref[k0 + m:k0 + m + CONV_TILES, :, c0:c0 + LANES]
            else:
                src = shift_ref[s - 1, k0 + m:k0 + m + CONV_TILES]
            acc = acc + src * dww_ref[j, :, c0:c0 + LANES]
        y_ref[k0:k0 + CONV_TILES, :, c0:c0 + LANES] = acc


def _mixffn_kernel(a_ref, o_ref, x_ref, mod_ref, dww_ref, dwb_ref, cg_ref, cb_ref, wout_ref,
                   lg1_ref, lb1_ref, w1_ref, w3_ref, w2_ref, lg2_ref, lb2_ref, xo_ref,
                   conv_ref, pad_ref, shift_ref, y_ref, *, tm, nt):
    i = pl.program_id(0)
    n_steps = pl.num_programs(0)
    t = jnp.minimum(i, n_steps - 2) % nt

    @pl.when(i == 0)
    def _():
        conv_ref[1] = jnp.zeros((tm, CONV_WIDTH), BF16)

    mixed = jnp.concatenate([o_ref[...], conv_ref[(i + 1) % 2]], axis=-1)
    y = jnp.dot(mixed, wout_ref[...], preferred_element_type=F32)
    xm = _deepnorm(x_ref[...], mod_ref[2:3, :], y, lg1_ref[...], lb1_ref[...])
    h = (xm * (1.0 + mod_ref[4:5, :]) + mod_ref[3:4, :]).astype(BF16)
    a1 = jnp.dot(h, w1_ref[...], preferred_element_type=F32)
    a3 = jnp.dot(h, w3_ref[...], preferred_element_type=F32)
    hid = (a1 * jax.nn.sigmoid(a1) * a3).astype(BF16)
    yf = jnp.dot(hid, w2_ref[...], preferred_element_type=F32)
    xo_ref[...] = _deepnorm(xm, mod_ref[5:6, :], yf, lg2_ref[...], lb2_ref[...])

    _conv_stage(a_ref, t, pad_ref, tm=tm, nt=nt)
    for c0 in range(0, CONV_WIDTH, LANES):
        _conv_column(c0, dww_ref, dwb_ref, pad_ref, shift_ref, y_ref)
    yn = _layer_norm(y_ref[...].reshape(tm, CONV_WIDTH), cg_ref[...], cb_ref[...])
    conv_ref[i % 2] = (yn * jax.nn.sigmoid(yn)).astype(BF16)


def _mixffn_call(a, o, x, mods, dww8, dwb, cg, cb, w_out, lg1, lb1, w1, w3, w2, lg2, lb2, *, tm, mod_row):
    n_seg, seg_len, _ = x.shape
    nt = seg_len // tm
    n_total = n_seg * nt
    prev = lambda i: jnp.maximum(i - 1, 0)
    row_spec = lambda width: pl.BlockSpec((tm, width), lambda i: (prev(i), 0))
    if mod_row is None:
        mod_spec = pl.BlockSpec((None, 6, D_MODEL), lambda i: (prev(i) // nt, 0, 0))
    else:
        mod_spec = pl.BlockSpec((None, 6, D_MODEL), lambda i: (mod_row, 0, 0))
    n_win = (tm + 2 * HALO) // SUBLANES
    out = pl.pallas_call(
        functools.partial(_mixffn_kernel, tm=tm, nt=nt),
        grid=(n_total + 1,),
        in_specs=[pl.BlockSpec((None, seg_len, CONV_WIDTH), lambda i: (jnp.minimum(i, n_total - 1) // nt, 0, 0)),
                  row_spec(ATTN_WIDTH), row_spec(D_MODEL), mod_spec,
                  _const_spec((CONV_KERNEL, SUBLANES, CONV_WIDTH)), _const_spec((1, CONV_WIDTH)),
                  _const_spec((1, CONV_WIDTH)), _const_spec((1, CONV_WIDTH)),
                  _const_spec((D_MODEL, D_MODEL)), _const_spec((1, D_MODEL)), _const_spec((1, D_MODEL)),
                  _const_spec((D_MODEL, D_FF)), _const_spec((D_MODEL, D_FF)), _const_spec((D_FF, D_MODEL)),
                  _const_spec((1, D_MODEL)), _const_spec((1, D_MODEL))],
        out_specs=row_spec(D_MODEL),
        out_shape=jax.ShapeDtypeStruct((n_seg * seg_len, D_MODEL), F32),
        scratch_shapes=[pltpu.VMEM((2, tm, CONV_WIDTH), BF16),
                        pltpu.VMEM((n_win, SUBLANES, CONV_WIDTH), F32),
                        pltpu.VMEM((SUBLANES - 1, n_win - 1, SUBLANES, LANES), F32),
                        pltpu.VMEM((tm // SUBLANES, SUBLANES, CONV_WIDTH), F32)],
        compiler_params=_params(1),
        name="mix_ffn",
    )(a, o.reshape(n_seg * seg_len, ATTN_WIDTH), x.reshape(n_seg * seg_len, D_MODEL), mods, dww8, dwb, cg, cb,
      w_out, lg1, lb1, w1, w3, w2, lg2, lb2)
    return out.reshape(n_seg, seg_len, D_MODEL)


def _rope_tables(n):
    rows = n // GRID_W
    r, col = jnp.meshgrid(jnp.arange(rows), jnp.arange(GRID_W), indexing="ij")
    pos = jnp.stack([r.reshape(-1), col.reshape(-1)], axis=-1).astype(F32)
    n_freq = HEAD_DIM // 4
    freqs = ROPE_THETA ** (-jnp.arange(n_freq, dtype=F32) / n_freq)
    ang = pos[:, :, None] * freqs
    cos, sin = jnp.cos(ang), jnp.sin(ang)
    cos_head = jnp.concatenate([cos[:, 0], cos[:, 0], cos[:, 1], cos[:, 1]], axis=-1)
    sin_head = jnp.concatenate([-sin[:, 0], sin[:, 0], -sin[:, 1], sin[:, 1]], axis=-1)
    reps = LANES // HEAD_DIM
    return jnp.tile(cos_head, (1, reps)), jnp.tile(sin_head, (1, reps))


def kernel(x, c, ctx, c_ctx, w_ada, b_ada, w_in, q_norm_g, k_norm_g, dw_w, dw_b, conv_ln_g, conv_ln_b,
           w_out, ln1_g, ln1_b, w_ff1, w_ff3, w_ff2, ln2_g, ln2_b):
    n_batch, n_lat, _ = x.shape
    n_ctx = ctx.shape[1]
    assert n_batch + 1 <= MOD_ROWS
    ctx_row = n_batch
    lat_tm, ctx_tm = 512, n_ctx

    c_all = jnp.concatenate(
        [c, c_ctx[None, :], jnp.zeros((MOD_ROWS - n_batch - 1, D_MODEL), F32)], axis=0)
    mods = _ada_call(c_all, w_ada, b_ada.reshape(DEPTH, 1, 6 * D_MODEL))
    mods = mods.reshape(DEPTH, MOD_ROWS, 6, D_MODEL)

    cos_l, sin_l = _rope_tables(n_lat)
    cos_c, sin_c = jnp.ones((n_ctx, LANES), F32), jnp.zeros((n_ctx, LANES), F32)
    head_id = jnp.arange(ATTN_WIDTH) // HEAD_DIM
    bd = (head_id[:, None] == head_id[None, :]).astype(BF16)
    bd_kv = bd[:KV_WIDTH, :KV_WIDTH]

    w_in_b, w_out_b = w_in.astype(BF16), w_out.astype(BF16)
    w1_b, w3_b, w2_b = w_ff1.astype(BF16), w_ff3.astype(BF16), w_ff2.astype(BF16)
    gq = jnp.tile(q_norm_g, (1, N_Q_HEADS)).reshape(DEPTH, 1, ATTN_WIDTH)
    gk = jnp.tile(k_norm_g, (1, N_KV_HEADS)).reshape(DEPTH, 1, KV_WIDTH)
    dww8 = jnp.broadcast_to(dw_w.reshape(DEPTH, CONV_KERNEL, 1, CONV_WIDTH),
                            (DEPTH, CONV_KERNEL, SUBLANES, CONV_WIDTH))
    row = lambda p, l: p[l].reshape(1, -1)

    for l in range(DEPTH):
        last = l == DEPTH - 1
        q_l, kk_l, vv_l, a_l = _inproj_call(x, mods[l], w_in_b[l], gq[l], gk[l], cos_l, sin_l, bd,
                                            tm=lat_tm, mod_row=None)
        if last:
            kk_c, vv_c = _kvproj_call(ctx, mods[l], w_in_b[l][:, ATTN_WIDTH:ATTN_WIDTH + 2 * KV_WIDTH],
                                      gk[l], cos_c, sin_c, bd_kv, tm=ctx_tm, mod_row=ctx_row)
        else:
            q_c, kk_c, vv_c, a_c = _inproj_call(ctx, mods[l], w_in_b[l], gq[l], gk[l], cos_c, sin_c, bd,
                                                tm=ctx_tm, mod_row=ctx_row)
        tail_args = (dww8[l], row(dw_b, l), row(conv_ln_g, l), row(conv_ln_b, l), w_out_b[l],
                     row(ln1_g, l), row(ln1_b, l), w1_b[l], w3_b[l], w2_b[l], row(ln2_g, l), row(ln2_b, l))

        o_l = _attn_call(q_l, [kk_c, vv_c, kk_l, vv_l], tq=ATTN_TILE_ROWS)
        x = _mixffn_call(a_l, o_l, x, mods[l], *tail_args, tm=lat_tm, mod_row=None)

        if not last:
            o_c = _attn_call(q_c, [kk_c, vv_c], tq=ctx_tm)
            ctx = _mixffn_call(a_c, o_c, ctx, mods[l], *tail_args, tm=ctx_tm, mod_row=ctx_row)
    return x
```

```python
import functools

import jax
import jax.numpy as jnp
from jax import lax
from jax.experimental import pallas as pl
from jax.experimental.pallas import tpu as pltpu

F32 = jnp.float32
BF16 = jnp.bfloat16

D_MODEL = 1024
DEPTH = 4
GRID_W = 64
HEAD_DIM = 64
N_Q_HEADS = 8
N_KV_HEADS = 2
ATTN_WIDTH = N_Q_HEADS * HEAD_DIM
KV_WIDTH = N_KV_HEADS * HEAD_DIM
CONV_WIDTH = D_MODEL - ATTN_WIDTH
CONV_KERNEL = 31
CONV_HALF = CONV_KERNEL // 2
D_FF = 2816
ROPE_THETA = 10000.0
DN_ALPHA = (2 * DEPTH) ** 0.25
LN_EPS = 1e-6
RMS_EPS = 1e-6
ATTN_SCALE = HEAD_DIM ** -0.5
IN_WIDTH = ATTN_WIDTH + 2 * KV_WIDTH + 2 * CONV_WIDTH

LANES = 128
SUBLANES = 8
HALO = 16
ATTN_KEY_BLOCK = 2048
ATTN_UNIT_ROWS = 512
ATTN_TILE_ROWS = 1024
CONV_TILES = 4
CHAIN_DELAY_ROLLS = 4
MOD_ROWS = 16
VMEM_LIMIT = 52 * 1024 * 1024

NT_DIMS = (((1,), (1,)), ((), ()))


def _const_spec(shape):
    zeros = (0,) * len(shape)
    return pl.BlockSpec(shape, lambda *_: zeros, pipeline_mode=pl.Buffered(1))


def _layer_norm(z, g, b, eps=LN_EPS):
    mu = jnp.mean(z, axis=-1, keepdims=True)
    zc = z - mu
    var = jnp.mean(zc * zc, axis=-1, keepdims=True)
    return zc * lax.rsqrt(var + eps) * g + b


def _deepnorm(x, gate, y, g, b):
    return _layer_norm(x + (gate * (1.0 / DN_ALPHA)) * y, g, b, eps=LN_EPS / DN_ALPHA ** 2)


def _params(n_axes=2):
    return pltpu.CompilerParams(dimension_semantics=("arbitrary",) * n_axes, vmem_limit_bytes=VMEM_LIMIT)


def _ada_kernel(c_ref, w_ref, b_ref, o_ref):
    c = c_ref[...]
    s = (c * jax.nn.sigmoid(c)).astype(BF16)
    o_ref[...] = jnp.dot(s, w_ref[...].astype(BF16), preferred_element_type=F32) + b_ref[...]


def _ada_call(c_all, w_ada, b_ada):
    n_col = 6 * D_MODEL // D_MODEL
    return pl.pallas_call(
        _ada_kernel,
        grid=(DEPTH, n_col),
        in_specs=[
            pl.BlockSpec((MOD_ROWS, D_MODEL), lambda l, j: (0, 0)),
            pl.BlockSpec((None, D_MODEL, D_MODEL), lambda l, j: (l, 0, j)),
            pl.BlockSpec((None, 1, D_MODEL), lambda l, j: (l, 0, j)),
        ],
        out_specs=pl.BlockSpec((None, MOD_ROWS, D_MODEL), lambda l, j: (l, 0, j)),
        out_shape=jax.ShapeDtypeStruct((DEPTH, MOD_ROWS, 6 * D_MODEL), F32),
        compiler_params=_params(),
        name="ada_mod",
    )(c_all, w_ada, b_ada)


def _head_norm_rope(z, gain, bd, cos, sin):
    ssq = jnp.dot((z * z).astype(BF16), bd, preferred_element_type=F32)
    zn = z * lax.rsqrt(ssq * (1.0 / HEAD_DIM) + RMS_EPS) * gain
    outs = []
    for c0 in range(0, z.shape[1], LANES):
        zc = zn[:, c0:c0 + LANES]
        up = pltpu.roll(zc, LANES - 16, 1)
        dn = pltpu.roll(zc, 16, 1)
        lane = lax.broadcasted_iota(jnp.int32, zc.shape, 1)
        partner = jnp.where((lane & 16) == 0, up, dn)
        outs.append(zc * cos + partner * sin)
    return outs


def _kv_variants(z, fill):
    zs = pltpu.roll(z, HEAD_DIM, 1)
    lo = lax.broadcasted_iota(jnp.int32, z.shape, 1) < HEAD_DIM
    zero = jnp.full_like(z, fill)
    return jnp.concatenate(
        [jnp.where(lo, z, zero), jnp.where(lo, zero, zs), jnp.where(lo, zs, zero), jnp.where(lo, zero, z)],
        axis=-1).astype(BF16)


def _inproj_kernel(x_ref, mod_ref, w_ref, gq_ref, gk_ref, cos_ref, sin_ref, bd_ref,
                   q_ref, kk_ref, vv_ref, a_ref):
    h = (x_ref[...] * (1.0 + mod_ref[1:2, :]) + mod_ref[0:1, :]).astype(BF16)
    proj = jnp.dot(h, w_ref[...], preferred_element_type=F32)
    cos, sin = cos_ref[...], sin_ref[...]
    q = proj[:, :ATTN_WIDTH]
    k = proj[:, ATTN_WIDTH:ATTN_WIDTH + KV_WIDTH]
    v = proj[:, ATTN_WIDTH + KV_WIDTH:ATTN_WIDTH + 2 * KV_WIDTH]
    u = proj[:, ATTN_WIDTH + 2 * KV_WIDTH:ATTN_WIDTH + 2 * KV_WIDTH + CONV_WIDTH]
    g = proj[:, ATTN_WIDTH + 2 * KV_WIDTH + CONV_WIDTH:]
    qs = _head_norm_rope(q, gq_ref[...], bd_ref[...], cos, sin)
    q_ref[...] = (jnp.concatenate(qs, axis=-1) * ATTN_SCALE).astype(BF16)
    (kr,) = _head_norm_rope(k, gk_ref[...], bd_ref[:KV_WIDTH, :KV_WIDTH], cos, sin)
    kk_ref[...] = _kv_variants(kr, 0.0)
    vv_ref[...] = _kv_variants(v, 1.0)
    a_ref[...] = (u * jax.nn.sigmoid(g)).astype(BF16)


def _kvproj_kernel(x_ref, mod_ref, w_ref, gk_ref, cos_ref, sin_ref, bd_ref, kk_ref, vv_ref):
    h = (x_ref[...] * (1.0 + mod_ref[1:2, :]) + mod_ref[0:1, :]).astype(BF16)
    proj = jnp.dot(h, w_ref[...], preferred_element_type=F32)
    (kr,) = _head_norm_rope(proj[:, :KV_WIDTH], gk_ref[...], bd_ref[...], cos_ref[...], sin_ref[...])
    kk_ref[...] = _kv_variants(kr, 0.0)
    vv_ref[...] = _kv_variants(proj[:, KV_WIDTH:], 1.0)


def _seg_spec(tm, width):
    return pl.BlockSpec((None, tm, width), lambda b, t: (b, t, 0))


def _mod_spec(mod_row):
    if mod_row is None:
        return pl.BlockSpec((None, 6, D_MODEL), lambda b, t: (b, 0, 0))
    return pl.BlockSpec((None, 6, D_MODEL), lambda b, t: (mod_row, 0, 0))


def _inproj_call(x, mods, w_in, gq, gk, cos, sin, bd, *, tm, mod_row):
    n_seg, seg_len, _ = x.shape
    nt = seg_len // tm
    tab_spec = pl.BlockSpec((tm, LANES), lambda b, t: (t, 0))
    return pl.pallas_call(
        _inproj_kernel,
        grid=(n_seg, nt),
        in_specs=[_seg_spec(tm, D_MODEL), _mod_spec(mod_row), _const_spec((D_MODEL, IN_WIDTH)),
                  _const_spec((1, ATTN_WIDTH)), _const_spec((1, KV_WIDTH)), tab_spec, tab_spec,
                  _const_spec((ATTN_WIDTH, ATTN_WIDTH))],
        out_specs=[_seg_spec(tm, ATTN_WIDTH), _seg_spec(tm, 4 * LANES), _seg_spec(tm, 4 * LANES),
                   _seg_spec(tm, CONV_WIDTH)],
        out_shape=[jax.ShapeDtypeStruct((n_seg, seg_len, ATTN_WIDTH), BF16),
                   jax.ShapeDtypeStruct((n_seg, seg_len, 4 * LANES), BF16),
                   jax.ShapeDtypeStruct((n_seg, seg_len, 4 * LANES), BF16),
                   jax.ShapeDtypeStruct((n_seg, seg_len, CONV_WIDTH), BF16)],
        compiler_params=_params(),
        name="in_proj",
    )(x, mods, w_in, gq, gk, cos, sin, bd)


def _kvproj_call(x, mods, w_kv, gk, cos, sin, bd_kv, *, tm, mod_row):
    n_seg, seg_len, _ = x.shape
    nt = seg_len // tm
    tab_spec = pl.BlockSpec((tm, LANES), lambda b, t: (t, 0))
    return pl.pallas_call(
        _kvproj_kernel,
        grid=(n_seg, nt),
        in_specs=[_seg_spec(tm, D_MODEL), _mod_spec(mod_row), _const_spec((D_MODEL, 2 * KV_WIDTH)),
                  _const_spec((1, KV_WIDTH)), tab_spec, tab_spec, _const_spec((KV_WIDTH, KV_WIDTH))],
        out_specs=[_seg_spec(tm, 4 * LANES), _seg_spec(tm, 4 * LANES)],
        out_shape=[jax.ShapeDtypeStruct((n_seg, seg_len, 4 * LANES), BF16),
                   jax.ShapeDtypeStruct((n_seg, seg_len, 4 * LANES), BF16)],
        compiler_params=_params(),
        name="kv_proj",
    )(x, mods, w_kv, gk, cos, sin, bd_kv)


def _attn_kernel(q_ref, *refs, n_src):
    o_ref = refs[2 * n_src]
    blocks = []
    for i in range(n_src):
        n_keys = refs[2 * i].shape[0]
        step = min(n_keys, ATTN_KEY_BLOCK)
        blocks += [(refs[2 * i], refs[2 * i + 1], k0, step) for k0 in range(0, n_keys, step)]
    for j in range(ATTN_WIDTH // LANES):
        kvh = j // 2
        rows = min(q_ref.shape[0], ATTN_UNIT_ROWS)
        for r0 in range(0, q_ref.shape[0], rows):
            qp = q_ref[r0:r0 + rows, j * LANES:(j + 1) * LANES]
            halves = []
            for par in range(2):
                c0 = (2 * kvh + par) * LANES
                scores = [lax.dot_general(qp, k_ref[k0:k0 + nk, c0:c0 + LANES], NT_DIMS,
                                          preferred_element_type=F32) for k_ref, _, k0, nk in blocks]
                m = scores[0].max(axis=-1, keepdims=True)
                for s in scores[1:]:
                    m = jnp.maximum(m, s.max(axis=-1, keepdims=True))
                o = None
                for (_, v_ref, k0, nk), s in zip(blocks, scores):
                    part = jnp.dot(jnp.exp(s - m).astype(BF16), v_ref[k0:k0 + nk, c0:c0 + LANES],
                                   preferred_element_type=F32)
                    o = part if o is None else o + part
                halves.append(o * (1.0 / pltpu.roll(o, HEAD_DIM, 1)))
            lo = lax.broadcasted_iota(jnp.int32, halves[0].shape, 1) < HEAD_DIM
            o_ref[r0:r0 + rows, j * LANES:(j + 1) * LANES] = jnp.where(lo, halves[0], halves[1]).astype(BF16)


def _attn_call(q, sources, *, tq):
    n_seg, seg_len, _ = q.shape
    kv_specs = [pl.BlockSpec((None, s.shape[1], 4 * LANES), lambda b, t: (b, 0, 0)) for s in sources]
    return pl.pallas_call(
        functools.partial(_attn_kernel, n_src=len(sources) // 2),
        grid=(n_seg, seg_len // tq),
        in_specs=[_seg_spec(tq, ATTN_WIDTH)] + kv_specs,
        out_specs=_seg_spec(tq, ATTN_WIDTH),
        out_shape=jax.ShapeDtypeStruct((n_seg, seg_len, ATTN_WIDTH), BF16),
        compiler_params=_params(),
        name="attention",
    )(q, *sources)


def _conv_stage(a_ref, t, pad_ref, *, tm, nt):
    seg_len = tm * nt
    zero_halo = jnp.zeros((HALO, CONV_WIDTH), F32)
    if nt == 1:
        start = 0
        top, bot = zero_halo, zero_halo
    else:
        start = pl.multiple_of(t * tm, tm)
        top_start = pl.multiple_of(jnp.maximum(start - HALO, 0), HALO)
        bot_start = pl.multiple_of(jnp.minimum(start + tm, seg_len - HALO), HALO)
        top = jnp.where(t == 0, zero_halo, a_ref[pl.ds(top_start, HALO), :].astype(F32))
        bot = jnp.where(t == nt - 1, zero_halo, a_ref[pl.ds(bot_start, HALO), :].astype(F32))
    halo_tiles = HALO // SUBLANES
    row_tiles = tm // SUBLANES
    pad_ref[0:halo_tiles] = top.reshape(halo_tiles, SUBLANES, CONV_WIDTH)
    pad_ref[halo_tiles:halo_tiles + row_tiles] = (
        a_ref[pl.ds(start, tm), :].astype(F32).reshape(row_tiles, SUBLANES, CONV_WIDTH))
    pad_ref[halo_tiles + row_tiles:] = bot.reshape(halo_tiles, SUBLANES, CONV_WIDTH)


def _ordered_after(x, dep):
    assert CHAIN_DELAY_ROLLS >= 2
    bits = lax.bitcast_convert_type(dep, jnp.uint32)
    for _ in range(CHAIN_DELAY_ROLLS):
        bits = pltpu.roll(bits, 1, 1) >> 16
    return x + lax.bitcast_convert_type(bits, F32)


def _conv_column(c0, dww_ref, dwb_ref, pad_ref, shift_ref, y_ref):
    n_tiles = pad_ref.shape[0]
    row_tiles = y_ref.shape[0]
    sub = lax.broadcasted_iota(jnp.int32, (n_tiles - 1, SUBLANES, LANES), 1)
    for s in range(1, SUBLANES):
        rot = pltpu.roll(pad_ref[:, :, c0:c0 + LANES], SUBLANES - s, 1)
        shift_ref[s - 1] = jnp.where(sub < SUBLANES - s, rot[:-1], rot[1:])
    bias = jnp.broadcast_to(dwb_ref[:, c0:c0 + LANES], (SUBLANES, LANES))
    acc = None
    for k0 in range(0, row_tiles, CONV_TILES):
        init = bias if acc is None else _ordered_after(bias, acc[0])
        acc = jnp.broadcast_to(init, (CONV_TILES, SUBLANES, LANES))
        for s in range(SUBLANES):
            taps = [j for j in range(CONV_KERNEL) if (HALO - CONV_HALF + j) % SUBLANES == s]
            m_hi = (HALO - CONV_HALF + taps[-1]) // SUBLANES
            if s == 0:
                span = pad_ref[k0:k0 + m_hi + CONV_TILES, :, c0:c0 + LANES]
            else:
                span = shift_ref[s - 1, k0:k0 + m_hi + CONV_TILES]
            for j in taps:
                m = (HALO - CONV_HALF + j) // SUBLANES
                acc = acc + span[m:m + CONV_TILES] * dww_ref[j, :, c0:c0 + LANES]
        y_ref[k0:k0 + CONV_TILES, :, c0:c0 + LANES] = acc


def _mixffn_kernel(a_ref, o_ref, x_ref, mod_ref, dww_ref, dwb_ref, cg_ref, cb_ref, wout_ref,
                   lg1_ref, lb1_ref, w1_ref, w3_ref, w2_ref, lg2_ref, lb2_ref, xo_ref,
                   conv_ref, pad_ref, shift_ref, y_ref, *, tm, nt):
    i = pl.program_id(0)
    n_steps = pl.num_programs(0)
    t = jnp.minimum(i, n_steps - 2) % nt

    @pl.when(i == 0)
    def _():
        conv_ref[1] = jnp.zeros((tm, CONV_WIDTH), BF16)

    mixed = jnp.concatenate([o_ref[...], conv_ref[(i + 1) % 2]], axis=-1)
    y = jnp.dot(mixed, wout_ref[...], preferred_element_type=F32)
    xm = _deepnorm(x_ref[...], mod_ref[2:3, :], y, lg1_ref[...], lb1_ref[...])
    h = (xm * (1.0 + mod_ref[4:5, :]) + mod_ref[3:4, :]).astype(BF16)
    a1 = jnp.dot(h, w1_ref[...], preferred_element_type=F32)
    a3 = jnp.dot(h, w3_ref[...], preferred_element_type=F32)
    hid = (a1 * jax.nn.sigmoid(a1) * a3).astype(BF16)
    yf = jnp.dot(hid, w2_ref[...], preferred_element_type=F32)
    xo_ref[...] = _deepnorm(xm, mod_ref[5:6, :], yf, lg2_ref[...], lb2_ref[...])

    _conv_stage(a_ref, t, pad_ref, tm=tm, nt=nt)
    for c0 in range(0, CONV_WIDTH, LANES):
        _conv_column(c0, dww_ref, dwb_ref, pad_ref, shift_ref, y_ref)
    yn = _layer_norm(y_ref[...].reshape(tm, CONV_WIDTH), cg_ref[...], cb_ref[...])
    conv_ref[i % 2] = (yn * jax.nn.sigmoid(yn)).astype(BF16)


def _mixffn_call(a, o, x, mods, dww8, dwb, cg, cb, w_out, lg1, lb1, w1, w3, w2, lg2, lb2, *, tm, mod_row):
    n_seg, seg_len, _ = x.shape
    nt = seg_len // tm
    n_total = n_seg * nt
    prev = lambda i: jnp.maximum(i - 1, 0)
    row_spec = lambda width: pl.BlockSpec((tm, width), lambda i: (prev(i), 0))
    if mod_row is None:
        mod_spec = pl.BlockSpec((None, 6, D_MODEL), lambda i: (prev(i) // nt, 0, 0))
    else:
        mod_spec = pl.BlockSpec((None, 6, D_MODEL), lambda i: (mod_row, 0, 0))
    n_win = (tm + 2 * HALO) // SUBLANES
    out = pl.pallas_call(
        functools.partial(_mixffn_kernel, tm=tm, nt=nt),
        grid=(n_total + 1,),
        in_specs=[pl.BlockSpec((None, seg_len, CONV_WIDTH), lambda i: (jnp.minimum(i, n_total - 1) // nt, 0, 0)),
                  row_spec(ATTN_WIDTH), row_spec(D_MODEL), mod_spec,
                  _const_spec((CONV_KERNEL, SUBLANES, CONV_WIDTH)), _const_spec((1, CONV_WIDTH)),
                  _const_spec((1, CONV_WIDTH)), _const_spec((1, CONV_WIDTH)),
                  _const_spec((D_MODEL, D_MODEL)), _const_spec((1, D_MODEL)), _const_spec((1, D_MODEL)),
                  _const_spec((D_MODEL, D_FF)), _const_spec((D_MODEL, D_FF)), _const_spec((D_FF, D_MODEL)),
                  _const_spec((1, D_MODEL)), _const_spec((1, D_MODEL))],
        out_specs=row_spec(D_MODEL),
        out_shape=jax.ShapeDtypeStruct((n_seg * seg_len, D_MODEL), F32),
        scratch_shapes=[pltpu.VMEM((2, tm, CONV_WIDTH), BF16),
                        pltpu.VMEM((n_win, SUBLANES, CONV_WIDTH), F32),
                        pltpu.VMEM((SUBLANES - 1, n_win - 1, SUBLANES, LANES), F32),
                        pltpu.VMEM((tm // SUBLANES, SUBLANES, CONV_WIDTH), F32)],
        compiler_params=_params(1),
        name="mix_ffn",
    )(a, o.reshape(n_seg * seg_len, ATTN_WIDTH), x.reshape(n_seg * seg_len, D_MODEL), mods, dww8, dwb, cg, cb,
      w_out, lg1, lb1, w1, w3, w2, lg2, lb2)
    return out.reshape(n_seg, seg_len, D_MODEL)


def _rope_tables(n):
    rows = n // GRID_W
    r, col = jnp.meshgrid(jnp.arange(rows), jnp.arange(GRID_W), indexing="ij")
    pos = jnp.stack([r.reshape(-1), col.reshape(-1)], axis=-1).astype(F32)
    n_freq = HEAD_DIM // 4
    freqs = ROPE_THETA ** (-jnp.arange(n_freq, dtype=F32) / n_freq)
    ang = pos[:, :, None] * freqs
    cos, sin = jnp.cos(ang), jnp.sin(ang)
    cos_head = jnp.concatenate([cos[:, 0], cos[:, 0], cos[:, 1], cos[:, 1]], axis=-1)
    sin_head = jnp.concatenate([-sin[:, 0], sin[:, 0], -sin[:, 1], sin[:, 1]], axis=-1)
    reps = LANES // HEAD_DIM
    return jnp.tile(cos_head, (1, reps)), jnp.tile(sin_head, (1, reps))


def kernel(x, c, ctx, c_ctx, w_ada, b_ada, w_in, q_norm_g, k_norm_g, dw_w, dw_b, conv_ln_g, conv_ln_b,
           w_out, ln1_g, ln1_b, w_ff1, w_ff3, w_ff2, ln2_g, ln2_b):
    n_batch, n_lat, _ = x.shape
    n_ctx = ctx.shape[1]
    assert n_batch + 1 <= MOD_ROWS
    ctx_row = n_batch
    lat_tm, ctx_tm = 512, n_ctx

    c_all = jnp.concatenate(
        [c, c_ctx[None, :], jnp.zeros((MOD_ROWS - n_batch - 1, D_MODEL), F32)], axis=0)
    mods = _ada_call(c_all, w_ada, b_ada.reshape(DEPTH, 1, 6 * D_MODEL))
    mods = mods.reshape(DEPTH, MOD_ROWS, 6, D_MODEL)

    cos_l, sin_l = _rope_tables(n_lat)
    cos_c, sin_c = jnp.ones((n_ctx, LANES), F32), jnp.zeros((n_ctx, LANES), F32)
    head_id = jnp.arange(ATTN_WIDTH) // HEAD_DIM
    bd = (head_id[:, None] == head_id[None, :]).astype(BF16)
    bd_kv = bd[:KV_WIDTH, :KV_WIDTH]

    w_in_b, w_out_b = w_in.astype(BF16), w_out.astype(BF16)
    w1_b, w3_b, w2_b = w_ff1.astype(BF16), w_ff3.astype(BF16), w_ff2.astype(BF16)
    gq = jnp.tile(q_norm_g, (1, N_Q_HEADS)).reshape(DEPTH, 1, ATTN_WIDTH)
    gk = jnp.tile(k_norm_g, (1, N_KV_HEADS)).reshape(DEPTH, 1, KV_WIDTH)
    dww8 = jnp.broadcast_to(dw_w.reshape(DEPTH, CONV_KERNEL, 1, CONV_WIDTH),
                            (DEPTH, CONV_KERNEL, SUBLANES, CONV_WIDTH))
    row = lambda p, l: p[l].reshape(1, -1)

    for l in range(DEPTH):
        last = l == DEPTH - 1
        q_l, kk_l, vv_l, a_l = _inproj_call(x, mods[l], w_in_b[l], gq[l], gk[l], cos_l, sin_l, bd,
                                            tm=lat_tm, mod_row=None)
        if last:
            kk_c, vv_c = _kvproj_call(ctx, mods[l], w_in_b[l][:, ATTN_WIDTH:ATTN_WIDTH + 2 * KV_WIDTH],
                                      gk[l], cos_c, sin_c, bd_kv, tm=ctx_tm, mod_row=ctx_row)
        else:
            q_c, kk_c, vv_c, a_c = _inproj_call(ctx, mods[l], w_in_b[l], gq[l], gk[l], cos_c, sin_c, bd,
                                                tm=ctx_tm, mod_row=ctx_row)
        tail_args = (dww8[l], row(dw_b, l), row(conv_ln_g, l), row(conv_ln_b, l), w_out_b[l],
                     row(ln1_g, l), row(ln1_b, l), w1_b[l], w3_b[l], w2_b[l], row(ln2_g, l), row(ln2_b, l))

        o_l = _attn_call(q_l, [kk_c, vv_c, kk_l, vv_l], tq=ATTN_TILE_ROWS)
        x = _mixffn_call(a_l, o_l, x, mods[l], *tail_args, tm=lat_tm, mod_row=None)

        if not last:
            o_c = _attn_call(q_c, [kk_c, vv_c], tq=ctx_tm)
            ctx = _mixffn_call(a_c, o_c, ctx, mods[l], *tail_args, tm=ctx_tm, mod_row=ctx_row)
    return x
```

```python
import functools

import jax
import jax.numpy as jnp
from jax import lax
from jax.experimental import pallas as pl
from jax.experimental.pallas import tpu as pltpu

F32 = jnp.float32
BF16 = jnp.bfloat16

D_MODEL = 1024
DEPTH = 4
GRID_W = 64
HEAD_DIM = 64
N_Q_HEADS = 8
N_KV_HEADS = 2
ATTN_WIDTH = N_Q_HEADS * HEAD_DIM
KV_WIDTH = N_KV_HEADS * HEAD_DIM
CONV_WIDTH = D_MODEL - ATTN_WIDTH
CONV_KERNEL = 31
CONV_HALF = CONV_KERNEL // 2
D_FF = 2816
N_MOD = 6
ROPE_THETA = 10000.0
DN_ALPHA = (2 * DEPTH) ** 0.25
LN_EPS = 1e-6
RMS_EPS = 1e-6
ATTN_SCALE = HEAD_DIM ** -0.5
IN_WIDTH = ATTN_WIDTH + 2 * KV_WIDTH + 2 * CONV_WIDTH

LANES = 128
SUBLANES = 8
HALO = 16
ATTN_UNIT_ROWS = 512
ATTN_TILE_ROWS = 1024
ROW_TILE = 512
CONV_TILES = 4
CHAIN_DELAY_ROLLS = 4
MOD_ROWS = 16
VMEM_LIMIT = 52 * 1024 * 1024

NT_DIMS = (((1,), (1,)), ((), ()))


def _const_spec(shape):
    zeros = (0,) * len(shape)
    return pl.BlockSpec(shape, lambda *_: zeros, pipeline_mode=pl.Buffered(1))


def _layer_norm(z, g, b, eps=LN_EPS):
    mu = jnp.mean(z, axis=-1, keepdims=True)
    zc = z - mu
    var = jnp.mean(zc * zc, axis=-1, keepdims=True)
    return zc * lax.rsqrt(var + eps) * g + b


def _deepnorm(x, gate, y, g, b):
    return _layer_norm(x + (gate * (1.0 / DN_ALPHA)) * y, g, b, eps=LN_EPS / DN_ALPHA ** 2)


def _params(n_axes=2):
    return pltpu.CompilerParams(dimension_semantics=("arbitrary",) * n_axes, vmem_limit_bytes=VMEM_LIMIT)


def _ada_kernel(c_ref, w_ref, b_ref, o_ref):
    c = c_ref[...]
    s = (c * jax.nn.sigmoid(c)).astype(BF16)
    o_ref[...] = jnp.dot(s, w_ref[...].astype(BF16), preferred_element_type=F32) + b_ref[...]


def _ada_call(c_all, w_ada, b_ada):
    return pl.pallas_call(
        _ada_kernel,
        grid=(DEPTH, N_MOD),
        in_specs=[
            pl.BlockSpec((MOD_ROWS, D_MODEL), lambda l, j: (0, 0)),
            pl.BlockSpec((None, D_MODEL, D_MODEL), lambda l, j: (l, 0, j)),
            pl.BlockSpec((None, 1, D_MODEL), lambda l, j: (l, 0, j)),
        ],
        out_specs=pl.BlockSpec((None, MOD_ROWS, D_MODEL), lambda l, j: (l, 0, j)),
        out_shape=jax.ShapeDtypeStruct((DEPTH, MOD_ROWS, N_MOD * D_MODEL), F32),
        compiler_params=_params(),
        name="ada_mod",
    )(c_all, w_ada, b_ada)


def _head_norm_rope(z, gain, bd, cos, sin):
    ssq = jnp.dot((z * z).astype(BF16), bd, preferred_element_type=F32)
    zn = z * lax.rsqrt(ssq * (1.0 / HEAD_DIM) + RMS_EPS) * gain
    outs = []
    for c0 in range(0, z.shape[1], LANES):
        zc = zn[:, c0:c0 + LANES]
        up = pltpu.roll(zc, LANES - 16, 1)
        dn = pltpu.roll(zc, 16, 1)
        lane = lax.broadcasted_iota(jnp.int32, zc.shape, 1)
        partner = jnp.where((lane & 16) == 0, up, dn)
        outs.append(zc * cos + partner * sin)
    return outs


def _kv_variants(z, fill):
    zs = pltpu.roll(z, HEAD_DIM, 1)
    lo = lax.broadcasted_iota(jnp.int32, z.shape, 1) < HEAD_DIM
    spare = jnp.full_like(z, fill)
    return jnp.concatenate(
        [jnp.where(lo, z, spare), jnp.where(lo, spare, zs), jnp.where(lo, zs, spare), jnp.where(lo, spare, z)],
        axis=-1).astype(BF16)


def _inproj_kernel(x_ref, mod_ref, w_ref, gq_ref, gk_ref, cos_ref, sin_ref, bd_ref,
                   q_ref, kk_ref, vv_ref, a_ref):
    h = (x_ref[...] * (1.0 + mod_ref[1:2, :]) + mod_ref[0:1, :]).astype(BF16)
    proj = jnp.dot(h, w_ref[...], preferred_element_type=F32)
    cos, sin = cos_ref[...], sin_ref[...]
    q = proj[:, :ATTN_WIDTH]
    k = proj[:, ATTN_WIDTH:ATTN_WIDTH + KV_WIDTH]
    v = proj[:, ATTN_WIDTH + KV_WIDTH:ATTN_WIDTH + 2 * KV_WIDTH]
    u = proj[:, ATTN_WIDTH + 2 * KV_WIDTH:ATTN_WIDTH + 2 * KV_WIDTH + CONV_WIDTH]
    g = proj[:, ATTN_WIDTH + 2 * KV_WIDTH + CONV_WIDTH:]
    qs = _head_norm_rope(q, gq_ref[...], bd_ref[...], cos, sin)
    q_ref[...] = (jnp.concatenate(qs, axis=-1) * ATTN_SCALE).astype(BF16)
    (kr,) = _head_norm_rope(k, gk_ref[...], bd_ref[:KV_WIDTH, :KV_WIDTH], cos, sin)
    kk_ref[...] = _kv_variants(kr, 0.0)
    vv_ref[...] = _kv_variants(v, 1.0)
    a_ref[...] = (u * jax.nn.sigmoid(g)).astype(BF16)


def _kvproj_kernel(x_ref, mod_ref, w_ref, gk_ref, cos_ref, sin_ref, bd_ref, kk_ref, vv_ref):
    h = (x_ref[...] * (1.0 + mod_ref[1:2, :]) + mod_ref[0:1, :]).astype(BF16)
    proj = jnp.dot(h, w_ref[...], preferred_element_type=F32)
    (kr,) = _head_norm_rope(proj[:, :KV_WIDTH], gk_ref[...], bd_ref[...], cos_ref[...], sin_ref[...])
    kk_ref[...] = _kv_variants(kr, 0.0)
    vv_ref[...] = _kv_variants(proj[:, KV_WIDTH:], 1.0)


def _seg_spec(tm, width):
    return pl.BlockSpec((None, tm, width), lambda b, t: (b, t, 0))


def _mod_spec(mod_row):
    if mod_row is None:
        return pl.BlockSpec((None, N_MOD, D_MODEL), lambda b, t: (b, 0, 0))
    return pl.BlockSpec((None, N_MOD, D_MODEL), lambda b, t: (mod_row, 0, 0))


def _inproj_call(x, mods, w_in, gq, gk, cos, sin, bd, *, tm, mod_row):
    n_seg, seg_len, _ = x.shape
    nt = seg_len // tm
    tab_spec = pl.BlockSpec((tm, LANES), lambda b, t: (t, 0))
    return pl.pallas_call(
        _inproj_kernel,
        grid=(n_seg, nt),
        in_specs=[_seg_spec(tm, D_MODEL), _mod_spec(mod_row), _const_spec((D_MODEL, IN_WIDTH)),
                  _const_spec((1, ATTN_WIDTH)), _const_spec((1, KV_WIDTH)), tab_spec, tab_spec,
                  _const_spec((ATTN_WIDTH, ATTN_WIDTH))],
        out_specs=[_seg_spec(tm, ATTN_WIDTH), _seg_spec(tm, 4 * LANES), _seg_spec(tm, 4 * LANES),
                   _seg_spec(tm, CONV_WIDTH)],
        out_shape=[jax.ShapeDtypeStruct((n_seg, seg_len, ATTN_WIDTH), BF16),
                   jax.ShapeDtypeStruct((n_seg, seg_len, 4 * LANES), BF16),
                   jax.ShapeDtypeStruct((n_seg, seg_len, 4 * LANES), BF16),
                   jax.ShapeDtypeStruct((n_seg, seg_len, CONV_WIDTH), BF16)],
        compiler_params=_params(),
        name="in_proj",
    )(x, mods, w_in, gq, gk, cos, sin, bd)


def _kvproj_call(x, mods, w_kv, gk, cos, sin, bd_kv, *, tm, mod_row):
    n_seg, seg_len, _ = x.shape
    nt = seg_len // tm
    tab_spec = pl.BlockSpec((tm, LANES), lambda b, t: (t, 0))
    return pl.pallas_call(
        _kvproj_kernel,
        grid=(n_seg, nt),
        in_specs=[_seg_spec(tm, D_MODEL), _mod_spec(mod_row), _const_spec((D_MODEL, 2 * KV_WIDTH)),
                  _const_spec((1, KV_WIDTH)), tab_spec, tab_spec, _const_spec((KV_WIDTH, KV_WIDTH))],
        out_specs=[_seg_spec(tm, 4 * LANES), _seg_spec(tm, 4 * LANES)],
        out_shape=[jax.ShapeDtypeStruct((n_seg, seg_len, 4 * LANES), BF16),
                   jax.ShapeDtypeStruct((n_seg, seg_len, 4 * LANES), BF16)],
        compiler_params=_params(),
        name="kv_proj",
    )(x, mods, w_kv, gk, cos, sin, bd_kv)


def _attn_kernel(q_ref, *refs, n_src):
    sources = [(refs[2 * i], refs[2 * i + 1]) for i in range(n_src)]
    o_ref = refs[2 * n_src]
    rows = min(q_ref.shape[0], ATTN_UNIT_ROWS)
    for j in range(ATTN_WIDTH // LANES):
        kvh = j // 2
        for r0 in range(0, q_ref.shape[0], rows):
            qp = q_ref[r0:r0 + rows, j * LANES:(j + 1) * LANES]
            halves = []
            for par in range(2):
                c0 = (2 * kvh + par) * LANES
                scores = [lax.dot_general(qp, k_ref[:, c0:c0 + LANES], NT_DIMS, preferred_element_type=F32)
                          for k_ref, _ in sources]
                m = scores[0].max(axis=-1, keepdims=True)
                for s in scores[1:]:
                    m = jnp.maximum(m, s.max(axis=-1, keepdims=True))
                o = None
                for (_, v_ref), s in zip(sources, scores):
                    part = jnp.dot(jnp.exp(s - m).astype(BF16), v_ref[:, c0:c0 + LANES],
                                   preferred_element_type=F32)
                    o = part if o is None else o + part
                halves.append(o * (1.0 / pltpu.roll(o, HEAD_DIM, 1)))
            lo = lax.broadcasted_iota(jnp.int32, halves[0].shape, 1) < HEAD_DIM
            o_ref[r0:r0 + rows, j * LANES:(j + 1) * LANES] = jnp.where(lo, halves[0], halves[1]).astype(BF16)


def _attn_call(q, sources, *, tq):
    n_seg, seg_len, _ = q.shape
    kv_specs = [pl.BlockSpec((None, s.shape[1], 4 * LANES), lambda b, t: (b, 0, 0)) for s in sources]
    return pl.pallas_call(
        functools.partial(_attn_kernel, n_src=len(sources) // 2),
        grid=(n_seg, seg_len // tq),
        in_specs=[_seg_spec(tq, ATTN_WIDTH)] + kv_specs,
        out_specs=_seg_spec(tq, ATTN_WIDTH),
        out_shape=jax.ShapeDtypeStruct((n_seg, seg_len, ATTN_WIDTH), BF16),
        compiler_params=_params(),
        name="attention",
    )(q, *sources)


def _conv_stage(a_ref, t, pad_ref, *, tm, nt):
    seg_len = tm * nt
    zero_halo = jnp.zeros((HALO, CONV_WIDTH), F32)
    if nt == 1:
        start = 0
        top, bot = zero_halo, zero_halo
    else:
        start = pl.multiple_of(t * tm, tm)
        top_start = pl.multiple_of(jnp.maximum(start - HALO, 0), HALO)
        bot_start = pl.multiple_of(jnp.minimum(start + tm, seg_len - HALO), HALO)
        top = jnp.where(t == 0, zero_halo, a_ref[pl.ds(top_start, HALO), :].astype(F32))
        bot = jnp.where(t == nt - 1, zero_halo, a_ref[pl.ds(bot_start, HALO), :].astype(F32))
    halo_tiles = HALO // SUBLANES
    row_tiles = tm // SUBLANES
    pad_ref[0:halo_tiles] = top.reshape(halo_tiles, SUBLANES, CONV_WIDTH)
    pad_ref[halo_tiles:halo_tiles + row_tiles] = (
        a_ref[pl.ds(start, tm), :].astype(F32).reshape(row_tiles, SUBLANES, CONV_WIDTH))
    pad_ref[halo_tiles + row_tiles:] = bot.reshape(halo_tiles, SUBLANES, CONV_WIDTH)


def _ordered_after(x, dep):
    assert CHAIN_DELAY_ROLLS >= 2
    bits = lax.bitcast_convert_type(dep, jnp.uint32)
    for _ in range(CHAIN_DELAY_ROLLS):
        bits = pltpu.roll(bits, 1, 1) >> 16
    return x + lax.bitcast_convert_type(bits, F32)


def _conv_column(c0, dww_ref, dwb_ref, pad_ref, shift_ref, y_ref):
    n_tiles = pad_ref.shape[0]
    row_tiles = y_ref.shape[0]
    sub = lax.broadcasted_iota(jnp.int32, (n_tiles - 1, SUBLANES, LANES), 1)
    for s in range(1, SUBLANES):
        rot = pltpu.roll(pad_ref[:, :, c0:c0 + LANES], SUBLANES - s, 1)
        shift_ref[s - 1] = jnp.where(sub < SUBLANES - s, rot[:-1], rot[1:])
    bias = jnp.broadcast_to(dwb_ref[:, c0:c0 + LANES], (SUBLANES, LANES))
    acc = None
    for k0 in range(0, row_tiles, CONV_TILES):
        init = bias if acc is None else _ordered_after(bias, acc[0])
        acc = jnp.broadcast_to(init, (CONV_TILES, SUBLANES, LANES))
        for j in range(CONV_KERNEL):
            m, s = divmod(HALO - CONV_HALF + j, SUBLANES)
            if s == 0:
                src = pad_ref[k0 + m:k0 + m + CONV_TILES, :, c0:c0 + LANES]
            else:
                src = shift_ref[s - 1, k0 + m:k0 + m + CONV_TILES]
            acc = acc + src * dww_ref[j, :, c0:c0 + LANES]
        y_ref[k0:k0 + CONV_TILES, :, c0:c0 + LANES] = acc


def _mixffn_kernel(a_ref, o_ref, x_ref, mod_ref, dww_ref, dwb_ref, cg_ref, cb_ref, wout_ref,
                   lg1_ref, lb1_ref, w1_ref, w3_ref, w2_ref, lg2_ref, lb2_ref, xo_ref,
                   conv_ref, pad_ref, shift_ref, y_ref, *, tm, nt):
    i = pl.program_id(0)
    n_steps = pl.num_programs(0)
    t = jnp.minimum(i, n_steps - 2) % nt

    @pl.when(i == 0)
    def _():
        conv_ref[1] = jnp.zeros((tm, CONV_WIDTH), BF16)

    mixed = jnp.concatenate([o_ref[...], conv_ref[(i + 1) % 2]], axis=-1)
    y = jnp.dot(mixed, wout_ref[...], preferred_element_type=F32)
    xm = _deepnorm(x_ref[...], mod_ref[2:3, :], y, lg1_ref[...], lb1_ref[...])
    h = (xm * (1.0 + mod_ref[4:5, :]) + mod_ref[3:4, :]).astype(BF16)
    a1 = jnp.dot(h, w1_ref[...], preferred_element_type=F32)
    a3 = jnp.dot(h, w3_ref[...], preferred_element_type=F32)
    hid = (a1 * jax.nn.sigmoid(a1) * a3).astype(BF16)
    yf = jnp.dot(hid, w2_ref[...], preferred_element_type=F32)
    xo_ref[...] = _deepnorm(xm, mod_ref[5:6, :], yf, lg2_ref[...], lb2_ref[...])

    _conv_stage(a_ref, t, pad_ref, tm=tm, nt=nt)
    for c0 in range(0, CONV_WIDTH, LANES):
        _conv_column(c0, dww_ref, dwb_ref, pad_ref, shift_ref, y_ref)
    yn = _layer_norm(y_ref[...].reshape(tm, CONV_WIDTH), cg_ref[...], cb_ref[...])
    conv_ref[i % 2] = (yn * jax.nn.sigmoid(yn)).astype(BF16)


def _mixffn_call(a, o, x, mods, dww8, dwb, cg, cb, w_out, lg1, lb1, w1, w3, w2, lg2, lb2, *, tm, mod_row):
    n_seg, seg_len, _ = x.shape
    nt = seg_len // tm
    n_total = n_seg * nt
    prev = lambda i: jnp.maximum(i - 1, 0)
    row_spec = lambda width: pl.BlockSpec((tm, width), lambda i: (prev(i), 0))
    if mod_row is None:
        mod_spec = pl.BlockSpec((None, N_MOD, D_MODEL), lambda i: (prev(i) // nt, 0, 0))
    else:
        mod_spec = pl.BlockSpec((None, N_MOD, D_MODEL), lambda i: (mod_row, 0, 0))
    n_win = (tm + 2 * HALO) // SUBLANES
    out = pl.pallas_call(
        functools.partial(_mixffn_kernel, tm=tm, nt=nt),
        grid=(n_total + 1,),
        in_specs=[pl.BlockSpec((None, seg_len, CONV_WIDTH), lambda i: (jnp.minimum(i, n_total - 1) // nt, 0, 0)),
                  row_spec(ATTN_WIDTH), row_spec(D_MODEL), mod_spec,
                  _const_spec((CONV_KERNEL, SUBLANES, CONV_WIDTH)), _const_spec((1, CONV_WIDTH)),
                  _const_spec((1, CONV_WIDTH)), _const_spec((1, CONV_WIDTH)),
                  _const_spec((D_MODEL, D_MODEL)), _const_spec((1, D_MODEL)), _const_spec((1, D_MODEL)),
                  _const_spec((D_MODEL, D_FF)), _const_spec((D_MODEL, D_FF)), _const_spec((D_FF, D_MODEL)),
                  _const_spec((1, D_MODEL)), _const_spec((1, D_MODEL))],
        out_specs=row_spec(D_MODEL),
        out_shape=jax.ShapeDtypeStruct((n_seg * seg_len, D_MODEL), F32),
        scratch_shapes=[pltpu.VMEM((2, tm, CONV_WIDTH), BF16),
                        pltpu.VMEM((n_win, SUBLANES, CONV_WIDTH), F32),
                        pltpu.VMEM((SUBLANES - 1, n_win - 1, SUBLANES, LANES), F32),
                        pltpu.VMEM((tm // SUBLANES, SUBLANES, CONV_WIDTH), F32)],
        compiler_params=_params(1),
        name="mix_ffn",
    )(a, o.reshape(n_seg * seg_len, ATTN_WIDTH), x.reshape(n_seg * seg_len, D_MODEL), mods, dww8, dwb, cg, cb,
      w_out, lg1, lb1, w1, w3, w2, lg2, lb2)
    return out.reshape(n_seg, seg_len, D_MODEL)


def _rope_tables(n):
    rows = n // GRID_W
    r, col = jnp.meshgrid(jnp.arange(rows), jnp.arange(GRID_W), indexing="ij")
    pos = jnp.stack([r.reshape(-1), col.reshape(-1)], axis=-1).astype(F32)
    n_freq = HEAD_DIM // 4
    freqs = ROPE_THETA ** (-jnp.arange(n_freq, dtype=F32) / n_freq)
    ang = pos[:, :, None] * freqs
    cos, sin = jnp.cos(ang), jnp.sin(ang)
    cos_head = jnp.concatenate([cos[:, 0], cos[:, 0], cos[:, 1], cos[:, 1]], axis=-1)
    sin_head = jnp.concatenate([-sin[:, 0], sin[:, 0], -sin[:, 1], sin[:, 1]], axis=-1)
    reps = LANES // HEAD_DIM
    return jnp.tile(cos_head, (1, reps)), jnp.tile(sin_head, (1, reps))


def kernel(x, c, ctx, c_ctx, w_ada, b_ada, w_in, q_norm_g, k_norm_g, dw_w, dw_b, conv_ln_g, conv_ln_b,
           w_out, ln1_g, ln1_b, w_ff1, w_ff3, w_ff2, ln2_g, ln2_b):
    n_batch, n_lat, _ = x.shape
    n_ctx = ctx.shape[1]
    assert n_batch + 1 <= MOD_ROWS
    assert n_lat % ATTN_TILE_ROWS == 0 and n_lat % ROW_TILE == 0 and n_ctx % HALO == 0
    ctx_row = n_batch
    lat_tm, ctx_tm = ROW_TILE, n_ctx

    c_all = jnp.concatenate(
        [c, c_ctx[None, :], jnp.zeros((MOD_ROWS - n_batch - 1, D_MODEL), F32)], axis=0)
    mods = _ada_call(c_all, w_ada, b_ada.reshape(DEPTH, 1, N_MOD * D_MODEL))
    mods = mods.reshape(DEPTH, MOD_ROWS, N_MOD, D_MODEL)

    cos_l, sin_l = _rope_tables(n_lat)
    cos_c, sin_c = jnp.ones((n_ctx, LANES), F32), jnp.zeros((n_ctx, LANES), F32)
    head_id = jnp.arange(ATTN_WIDTH) // HEAD_DIM
    bd = (head_id[:, None] == head_id[None, :]).astype(BF16)
    bd_kv = bd[:KV_WIDTH, :KV_WIDTH]

    w_in_b, w_out_b = w_in.astype(BF16), w_out.astype(BF16)
    w1_b, w3_b, w2_b = w_ff1.astype(BF16), w_ff3.astype(BF16), w_ff2.astype(BF16)
    gq = jnp.tile(q_norm_g, (1, N_Q_HEADS)).reshape(DEPTH, 1, ATTN_WIDTH)
    gk = jnp.tile(k_norm_g, (1, N_KV_HEADS)).reshape(DEPTH, 1, KV_WIDTH)
    dww8 = jnp.broadcast_to(dw_w.reshape(DEPTH, CONV_KERNEL, 1, CONV_WIDTH),
                            (DEPTH, CONV_KERNEL, SUBLANES, CONV_WIDTH))
    row = lambda p, l: p[l].reshape(1, -1)

    for l in range(DEPTH):
        last = l == DEPTH - 1
        q_l, kk_l, vv_l, a_l = _inproj_call(x, mods[l], w_in_b[l], gq[l], gk[l], cos_l, sin_l, bd,
                                            tm=lat_tm, mod_row=None)
        if last:
            kk_c, vv_c = _kvproj_call(ctx, mods[l], w_in_b[l][:, ATTN_WIDTH:ATTN_WIDTH + 2 * KV_WIDTH],
                                      gk[l], cos_c, sin_c, bd_kv, tm=ctx_tm, mod_row=ctx_row)
        else:
            q_c, kk_c, vv_c, a_c = _inproj_call(ctx, mods[l], w_in_b[l], gq[l], gk[l], cos_c, sin_c, bd,
                                                tm=ctx_tm, mod_row=ctx_row)
        tail_args = (dww8[l], row(dw_b, l), row(conv_ln_g, l), row(conv_ln_b, l), w_out_b[l],
                     row(ln1_g, l), row(ln1_b, l), w1_b[l], w3_b[l], w2_b[l], row(ln2_g, l), row(ln2_b, l))

        o_l = _attn_call(q_l, [kk_c, vv_c, kk_l, vv_l], tq=ATTN_TILE_ROWS)
        x = _mixffn_call(a_l, o_l, x, mods[l], *tail_args, tm=lat_tm, mod_row=None)

        if not last:
            o_c = _attn_call(q_c, [kk_c, vv_c], tq=ctx_tm)
            ctx = _mixffn_call(a_c, o_c, ctx, mods[l], *tail_args, tm=ctx_tm, mod_row=ctx_row)
    return x
```

```python
import functools

import jax
import jax.numpy as jnp
from jax import lax
from jax.experimental import pallas as pl
from jax.experimental.pallas import tpu as pltpu

F32 = jnp.float32
BF16 = jnp.bfloat16

D_MODEL = 1024
DEPTH = 4
GRID_W = 64
HEAD_DIM = 64
N_Q_HEADS = 8
N_KV_HEADS = 2
ATTN_WIDTH = N_Q_HEADS * HEAD_DIM
KV_WIDTH = N_KV_HEADS * HEAD_DIM
CONV_WIDTH = D_MODEL - ATTN_WIDTH
CONV_KERNEL = 31
CONV_HALF = CONV_KERNEL // 2
D_FF = 2816
N_MOD = 6
ROPE_THETA = 10000.0
DN_ALPHA = (2 * DEPTH) ** 0.25
LN_EPS = 1e-6
RMS_EPS = 1e-6
ATTN_SCALE = HEAD_DIM ** -0.5
IN_WIDTH = ATTN_WIDTH + 2 * KV_WIDTH + 2 * CONV_WIDTH

LANES = 128
SUBLANES = 8
HALO = 16
ATTN_UNIT_ROWS = 512
ATTN_TILE_ROWS = 1024
ROW_TILE = 512
CONV_TILES = 4
CHAIN_DELAY_ROLLS = 4
MOD_ROWS = 16
VMEM_LIMIT = 52 * 1024 * 1024

NT_DIMS = (((1,), (1,)), ((), ()))


def _const_spec(shape):
    zeros = (0,) * len(shape)
    return pl.BlockSpec(shape, lambda *_: zeros, pipeline_mode=pl.Buffered(1))


def _layer_norm(z, g, b, eps=LN_EPS):
    mu = jnp.mean(z, axis=-1, keepdims=True)
    zc = z - mu
    var = jnp.mean(zc * zc, axis=-1, keepdims=True)
    return zc * lax.rsqrt(var + eps) * g + b


def _deepnorm(x, gate, y, g, b):
    return _layer_norm(x + (gate * (1.0 / DN_ALPHA)) * y, g, b, eps=LN_EPS / DN_ALPHA ** 2)


def _params(n_axes=2):
    return pltpu.CompilerParams(dimension_semantics=("arbitrary",) * n_axes, vmem_limit_bytes=VMEM_LIMIT)


def _ada_kernel(c_ref, w_ref, b_ref, o_ref):
    c = c_ref[...]
    s = (c * jax.nn.sigmoid(c)).astype(BF16)
    o_ref[...] = jnp.dot(s, w_ref[...].astype(BF16), preferred_element_type=F32) + b_ref[...]


def _ada_call(c_all, w_ada, b_ada):
    return pl.pallas_call(
        _ada_kernel,
        grid=(DEPTH, N_MOD),
        in_specs=[
            pl.BlockSpec((MOD_ROWS, D_MODEL), lambda l, j: (0, 0)),
            pl.BlockSpec((None, D_MODEL, D_MODEL), lambda l, j: (l, 0, j)),
            pl.BlockSpec((None, 1, D_MODEL), lambda l, j: (l, 0, j)),
        ],
        out_specs=pl.BlockSpec((None, MOD_ROWS, D_MODEL), lambda l, j: (l, 0, j)),
        out_shape=jax.ShapeDtypeStruct((DEPTH, MOD_ROWS, N_MOD * D_MODEL), F32),
        compiler_params=_params(),
        name="ada_mod",
    )(c_all, w_ada, b_ada)


def _head_norm_rope(z, gain, bd, cos, sin):
    ssq = jnp.dot((z * z).astype(BF16), bd, preferred_element_type=F32)
    zn = z * lax.rsqrt(ssq * (1.0 / HEAD_DIM) + RMS_EPS) * gain
    outs = []
    for c0 in range(0, z.shape[1], LANES):
        zc = zn[:, c0:c0 + LANES]
        up = pltpu.roll(zc, LANES - 16, 1)
        dn = pltpu.roll(zc, 16, 1)
        lane = lax.broadcasted_iota(jnp.int32, zc.shape, 1)
        partner = jnp.where((lane & 16) == 0, up, dn)
        outs.append(zc * cos + partner * sin)
    return outs


def _kv_variants(z, fill):
    zs = pltpu.roll(z, HEAD_DIM, 1)
    lo = lax.broadcasted_iota(jnp.int32, z.shape, 1) < HEAD_DIM
    spare = jnp.full_like(z, fill)
    return jnp.concatenate(
        [jnp.where(lo, z, spare), jnp.where(lo, spare, zs), jnp.where(lo, zs, spare), jnp.where(lo, spare, z)],
        axis=-1).astype(BF16)


def _inproj_kernel(x_ref, mod_ref, w_ref, gq_ref, gk_ref, cos_ref, sin_ref, bd_ref,
                   q_ref, kk_ref, vv_ref, a_ref):
    h = (x_ref[...] * (1.0 + mod_ref[1:2, :]) + mod_ref[0:1, :]).astype(BF16)
    proj = jnp.dot(h, w_ref[...], preferred_element_type=F32)
    cos, sin = cos_ref[...], sin_ref[...]
    q = proj[:, :ATTN_WIDTH]
    k = proj[:, ATTN_WIDTH:ATTN_WIDTH + KV_WIDTH]
    v = proj[:, ATTN_WIDTH + KV_WIDTH:ATTN_WIDTH + 2 * KV_WIDTH]
    u = proj[:, ATTN_WIDTH + 2 * KV_WIDTH:ATTN_WIDTH + 2 * KV_WIDTH + CONV_WIDTH]
    g = proj[:, ATTN_WIDTH + 2 * KV_WIDTH + CONV_WIDTH:]
    qs = _head_norm_rope(q, gq_ref[...], bd_ref[...], cos, sin)
    q_ref[...] = (jnp.concatenate(qs, axis=-1) * ATTN_SCALE).astype(BF16)
    (kr,) = _head_norm_rope(k, gk_ref[...], bd_ref[:KV_WIDTH, :KV_WIDTH], cos, sin)
    kk_ref[...] = _kv_variants(kr, 0.0)
    vv_ref[...] = _kv_variants(v, 1.0)
    a_ref[...] = (u * jax.nn.sigmoid(g)).astype(BF16)


def _kvproj_kernel(x_ref, mod_ref, w_ref, gk_ref, cos_ref, sin_ref, bd_ref, kk_ref, vv_ref):
    h = (x_ref[...] * (1.0 + mod_ref[1:2, :]) + mod_ref[0:1, :]).astype(BF16)
    proj = jnp.dot(h, w_ref[...], preferred_element_type=F32)
    (kr,) = _head_norm_rope(proj[:, :KV_WIDTH], gk_ref[...], bd_ref[...], cos_ref[...], sin_ref[...])
    kk_ref[...] = _kv_variants(kr, 0.0)
    vv_ref[...] = _kv_variants(proj[:, KV_WIDTH:], 1.0)


def _seg_spec(tm, width):
    return pl.BlockSpec((None, tm, width), lambda b, t: (b, t, 0))


def _mod_spec(mod_row):
    if mod_row is None:
        return pl.BlockSpec((None, N_MOD, D_MODEL), lambda b, t: (b, 0, 0))
    return pl.BlockSpec((None, N_MOD, D_MODEL), lambda b, t: (mod_row, 0, 0))


def _inproj_call(x, mods, w_in, gq, gk, cos, sin, bd, *, tm, mod_row):
    n_seg, seg_len, _ = x.shape
    nt = seg_len // tm
    tab_spec = pl.BlockSpec((tm, LANES), lambda b, t: (t, 0))
    return pl.pallas_call(
        _inproj_kernel,
        grid=(n_seg, nt),
        in_specs=[_seg_spec(tm, D_MODEL), _mod_spec(mod_row), _const_spec((D_MODEL, IN_WIDTH)),
                  _const_spec((1, ATTN_WIDTH)), _const_spec((1, KV_WIDTH)), tab_spec, tab_spec,
                  _const_spec((ATTN_WIDTH, ATTN_WIDTH))],
        out_specs=[_seg_spec(tm, ATTN_WIDTH), _seg_spec(tm, 4 * LANES), _seg_spec(tm, 4 * LANES),
                   _seg_spec(tm, CONV_WIDTH)],
        out_shape=[jax.ShapeDtypeStruct((n_seg, seg_len, ATTN_WIDTH), BF16),
                   jax.ShapeDtypeStruct((n_seg, seg_len, 4 * LANES), BF16),
                   jax.ShapeDtypeStruct((n_seg, seg_len, 4 * LANES), BF16),
                   jax.ShapeDtypeStruct((n_seg, seg_len, CONV_WIDTH), BF16)],
        compiler_params=_params(),
        name="in_proj",
    )(x, mods, w_in, gq, gk, cos, sin, bd)


def _kvproj_call(x, mods, w_kv, gk, cos, sin, bd_kv, *, tm, mod_row):
    n_seg, seg_len, _ = x.shape
    nt = seg_len // tm
    tab_spec = pl.BlockSpec((tm, LANES), lambda b, t: (t, 0))
    return pl.pallas_call(
        _kvproj_kernel,
        grid=(n_seg, nt),
        in_specs=[_seg_spec(tm, D_MODEL), _mod_spec(mod_row), _const_spec((D_MODEL, 2 * KV_WIDTH)),
                  _const_spec((1, KV_WIDTH)), tab_spec, tab_spec, _const_spec((KV_WIDTH, KV_WIDTH))],
        out_specs=[_seg_spec(tm, 4 * LANES), _seg_spec(tm, 4 * LANES)],
        out_shape=[jax.ShapeDtypeStruct((n_seg, seg_len, 4 * LANES), BF16),
                   jax.ShapeDtypeStruct((n_seg, seg_len, 4 * LANES), BF16)],
        compiler_params=_params(),
        name="kv_proj",
    )(x, mods, w_kv, gk, cos, sin, bd_kv)


def _attn_kernel(q_ref, *refs, n_src, n_cast):
    sources = [(refs[2 * i], refs[2 * i + 1]) for i in range(n_src)]
    cast_in = refs[2 * n_src:2 * n_src + n_cast]
    o_ref = refs[2 * n_src + n_cast]
    cast_out = refs[2 * n_src + n_cast + 1:]
    for w_ref, wb_ref in zip(cast_in, cast_out):
        wb_ref[...] = w_ref[...].astype(BF16)
    rows = min(q_ref.shape[0], ATTN_UNIT_ROWS)
    for j in range(ATTN_WIDTH // LANES):
        kvh = j // 2
        for r0 in range(0, q_ref.shape[0], rows):
            qp = q_ref[r0:r0 + rows, j * LANES:(j + 1) * LANES]
            halves = []
            for par in range(2):
                c0 = (2 * kvh + par) * LANES
                scores = [lax.dot_general(qp, k_ref[:, c0:c0 + LANES], NT_DIMS, preferred_element_type=F32)
                          for k_ref, _ in sources]
                m = scores[0].max(axis=-1, keepdims=True)
                for s in scores[1:]:
                    m = jnp.maximum(m, s.max(axis=-1, keepdims=True))
                o = None
                for (_, v_ref), s in zip(sources, scores):
                    part = jnp.dot(jnp.exp(s - m).astype(BF16), v_ref[:, c0:c0 + LANES],
                                   preferred_element_type=F32)
                    o = part if o is None else o + part
                halves.append(o * (1.0 / pltpu.roll(o, HEAD_DIM, 1)))
            lo = lax.broadcasted_iota(jnp.int32, halves[0].shape, 1) < HEAD_DIM
            o_ref[r0:r0 + rows, j * LANES:(j + 1) * LANES] = jnp.where(lo, halves[0], halves[1]).astype(BF16)


def _attn_call(q, sources, *, tq, cast_layer=None, cast_weights=()):
    n_seg, seg_len, _ = q.shape
    nt = seg_len // tq
    kv_specs = [pl.BlockSpec((None, s.shape[1], 4 * LANES), lambda b, t: (b, 0, 0)) for s in sources]
    cast_in_specs, cast_out_specs, cast_shapes = [], [], []
    for w in cast_weights:
        _, n_rows, n_cols = w.shape
        slab = n_rows // (n_seg * nt)
        assert slab * n_seg * nt == n_rows and slab % (2 * SUBLANES) == 0
        cast_in_specs.append(pl.BlockSpec((None, slab, n_cols), lambda b, t: (cast_layer, b * nt + t, 0)))
        cast_out_specs.append(pl.BlockSpec((slab, n_cols), lambda b, t: (b * nt + t, 0)))
        cast_shapes.append(jax.ShapeDtypeStruct((n_rows, n_cols), BF16))
    outs = pl.pallas_call(
        functools.partial(_attn_kernel, n_src=len(sources) // 2, n_cast=len(cast_weights)),
        grid=(n_seg, nt),
        in_specs=[_seg_spec(tq, ATTN_WIDTH)] + kv_specs + cast_in_specs,
        out_specs=[_seg_spec(tq, ATTN_WIDTH)] + cast_out_specs,
        out_shape=[jax.ShapeDtypeStruct((n_seg, seg_len, ATTN_WIDTH), BF16)] + cast_shapes,
        compiler_params=_params(),
        name="attention",
    )(q, *sources, *cast_weights)
    return outs[0], outs[1:]


def _conv_stage(a_ref, t, pad_ref, *, tm, nt):
    seg_len = tm * nt
    zero_halo = jnp.zeros((HALO, CONV_WIDTH), F32)
    if nt == 1:
        start = 0
        top, bot = zero_halo, zero_halo
    else:
        start = pl.multiple_of(t * tm, tm)
        top_start = pl.multiple_of(jnp.maximum(start - HALO, 0), HALO)
        bot_start = pl.multiple_of(jnp.minimum(start + tm, seg_len - HALO), HALO)
        top = jnp.where(t == 0, zero_halo, a_ref[pl.ds(top_start, HALO), :].astype(F32))
        bot = jnp.where(t == nt - 1, zero_halo, a_ref[pl.ds(bot_start, HALO), :].astype(F32))
    halo_tiles = HALO // SUBLANES
    row_tiles = tm // SUBLANES
    pad_ref[0:halo_tiles] = top.reshape(halo_tiles, SUBLANES, CONV_WIDTH)
    pad_ref[halo_tiles:halo_tiles + row_tiles] = (
        a_ref[pl.ds(start, tm), :].astype(F32).reshape(row_tiles, SUBLANES, CONV_WIDTH))
    pad_ref[halo_tiles + row_tiles:] = bot.reshape(halo_tiles, SUBLANES, CONV_WIDTH)


def _ordered_after(x, dep):
    assert CHAIN_DELAY_ROLLS >= 2
    bits = lax.bitcast_convert_type(dep, jnp.uint32)
    for _ in range(CHAIN_DELAY_ROLLS):
        bits = pltpu.roll(bits, 1, 1) >> 16
    return x + lax.bitcast_convert_type(bits, F32)


def _conv_column(c0, dww_ref, dwb_ref, pad_ref, shift_ref, y_ref):
    n_tiles = pad_ref.shape[0]
    row_tiles = y_ref.shape[0]
    sub = lax.broadcasted_iota(jnp.int32, (n_tiles - 1, SUBLANES, LANES), 1)
    for s in range(1, SUBLANES):
        rot = pltpu.roll(pad_ref[:, :, c0:c0 + LANES], SUBLANES - s, 1)
        shift_ref[s - 1] = jnp.where(sub < SUBLANES - s, rot[:-1], rot[1:])
    bias = jnp.broadcast_to(dwb_ref[:, c0:c0 + LANES], (SUBLANES, LANES))
    acc = None
    for k0 in range(0, row_tiles, CONV_TILES):
        init = bias if acc is None else _ordered_after(bias, acc[0])
        acc = jnp.broadcast_to(init, (CONV_TILES, SUBLANES, LANES))
        for j in range(CONV_KERNEL):
            m, s = divmod(HALO - CONV_HALF + j, SUBLANES)
            if s == 0:
                src = pad_ref[k0 + m:k0 + m + CONV_TILES, :, c0:c0 + LANES]
            else:
                src = shift_ref[s - 1, k0 + m:k0 + m + CONV_TILES]
            acc = acc + src * dww_ref[j, :, c0:c0 + LANES]
        y_ref[k0:k0 + CONV_TILES, :, c0:c0 + LANES] = acc


def _mixffn_kernel(a_ref, o_ref, x_ref, mod_ref, dww_ref, dwb_ref, cg_ref, cb_ref, wout_ref,
                   lg1_ref, lb1_ref, w1_ref, w3_ref, w2_ref, lg2_ref, lb2_ref, xo_ref,
                   conv_ref, pad_ref, shift_ref, y_ref, *, tm, nt):
    i = pl.program_id(0)
    n_steps = pl.num_programs(0)
    t = jnp.minimum(i, n_steps - 2) % nt

    @pl.when(i == 0)
    def _():
        conv_ref[1] = jnp.zeros((tm, CONV_WIDTH), BF16)

    mixed = jnp.concatenate([o_ref[...], conv_ref[(i + 1) % 2]], axis=-1)
    y = jnp.dot(mixed, wout_ref[...], preferred_element_type=F32)
    xm = _deepnorm(x_ref[...], mod_ref[2:3, :], y, lg1_ref[...], lb1_ref[...])
    h = (xm * (1.0 + mod_ref[4:5, :]) + mod_ref[3:4, :]).astype(BF16)
    a1 = jnp.dot(h, w1_ref[...], preferred_element_type=F32)
    a3 = jnp.dot(h, w3_ref[...], preferred_element_type=F32)
    hid = (a1 * jax.nn.sigmoid(a1) * a3).astype(BF16)
    yf = jnp.dot(hid, w2_ref[...], preferred_element_type=F32)
    xo_ref[...] = _deepnorm(xm, mod_ref[5:6, :], yf, lg2_ref[...], lb2_ref[...])

    _conv_stage(a_ref, t, pad_ref, tm=tm, nt=nt)
    for c0 in range(0, CONV_WIDTH, LANES):
        _conv_column(c0, dww_ref, dwb_ref, pad_ref, shift_ref, y_ref)
    yn = _layer_norm(y_ref[...].reshape(tm, CONV_WIDTH), cg_ref[...], cb_ref[...])
    conv_ref[i % 2] = (yn * jax.nn.sigmoid(yn)).astype(BF16)


def _mixffn_call(a, o, x, mods, dww8, dwb, cg, cb, w_out, lg1, lb1, w1, w3, w2, lg2, lb2, *, tm, mod_row):
    n_seg, seg_len, _ = x.shape
    nt = seg_len // tm
    n_total = n_seg * nt
    prev = lambda i: jnp.maximum(i - 1, 0)
    row_spec = lambda width: pl.BlockSpec((tm, width), lambda i: (prev(i), 0))
    if mod_row is None:
        mod_spec = pl.BlockSpec((None, N_MOD, D_MODEL), lambda i: (prev(i) // nt, 0, 0))
    else:
        mod_spec = pl.BlockSpec((None, N_MOD, D_MODEL), lambda i: (mod_row, 0, 0))
    n_win = (tm + 2 * HALO) // SUBLANES
    out = pl.pallas_call(
        functools.partial(_mixffn_kernel, tm=tm, nt=nt),
        grid=(n_total + 1,),
        in_specs=[pl.BlockSpec((None, seg_len, CONV_WIDTH), lambda i: (jnp.minimum(i, n_total - 1) // nt, 0, 0)),
                  row_spec(ATTN_WIDTH), row_spec(D_MODEL), mod_spec,
                  _const_spec((CONV_KERNEL, SUBLANES, CONV_WIDTH)), _const_spec((1, CONV_WIDTH)),
                  _const_spec((1, CONV_WIDTH)), _const_spec((1, CONV_WIDTH)),
                  _const_spec((D_MODEL, D_MODEL)), _const_spec((1, D_MODEL)), _const_spec((1, D_MODEL)),
                  _const_spec((D_MODEL, D_FF)), _const_spec((D_MODEL, D_FF)), _const_spec((D_FF, D_MODEL)),
                  _const_spec((1, D_MODEL)), _const_spec((1, D_MODEL))],
        out_specs=row_spec(D_MODEL),
        out_shape=jax.ShapeDtypeStruct((n_seg * seg_len, D_MODEL), F32),
        scratch_shapes=[pltpu.VMEM((2, tm, CONV_WIDTH), BF16),
                        pltpu.VMEM((n_win, SUBLANES, CONV_WIDTH), F32),
                        pltpu.VMEM((SUBLANES - 1, n_win - 1, SUBLANES, LANES), F32),
                        pltpu.VMEM((tm // SUBLANES, SUBLANES, CONV_WIDTH), F32)],
        compiler_params=_params(1),
        name="mix_ffn",
    )(a, o.reshape(n_seg * seg_len, ATTN_WIDTH), x.reshape(n_seg * seg_len, D_MODEL), mods, dww8, dwb, cg, cb,
      w_out, lg1, lb1, w1, w3, w2, lg2, lb2)
    return out.reshape(n_seg, seg_len, D_MODEL)


def _rope_tables(n):
    rows = n // GRID_W
    r, col = jnp.meshgrid(jnp.arange(rows), jnp.arange(GRID_W), indexing="ij")
    pos = jnp.stack([r.reshape(-1), col.reshape(-1)], axis=-1).astype(F32)
    n_freq = HEAD_DIM // 4
    freqs = ROPE_THETA ** (-jnp.arange(n_freq, dtype=F32) / n_freq)
    ang = pos[:, :, None] * freqs
    cos, sin = jnp.cos(ang), jnp.sin(ang)
    cos_head = jnp.concatenate([cos[:, 0], cos[:, 0], cos[:, 1], cos[:, 1]], axis=-1)
    sin_head = jnp.concatenate([-sin[:, 0], sin[:, 0], -sin[:, 1], sin[:, 1]], axis=-1)
    reps = LANES // HEAD_DIM
    return jnp.tile(cos_head, (1, reps)), jnp.tile(sin_head, (1, reps))


def kernel(x, c, ctx, c_ctx, w_ada, b_ada, w_in, q_norm_g, k_norm_g, dw_w, dw_b, conv_ln_g, conv_ln_b,
           w_out, ln1_g, ln1_b, w_ff1, w_ff3, w_ff2, ln2_g, ln2_b):
    n_batch, n_lat, _ = x.shape
    n_ctx = ctx.shape[1]
    assert n_batch + 1 <= MOD_ROWS
    assert n_lat % ATTN_TILE_ROWS == 0 and n_lat % ROW_TILE == 0 and n_ctx % HALO == 0
    ctx_row = n_batch
    lat_tm, ctx_tm = ROW_TILE, n_ctx

    c_all = jnp.concatenate(
        [c, c_ctx[None, :], jnp.zeros((MOD_ROWS - n_batch - 1, D_MODEL), F32)], axis=0)
    mods = _ada_call(c_all, w_ada, b_ada.reshape(DEPTH, 1, N_MOD * D_MODEL))
    mods = mods.reshape(DEPTH, MOD_ROWS, N_MOD, D_MODEL)

    cos_l, sin_l = _rope_tables(n_lat)
    cos_c, sin_c = jnp.ones((n_ctx, LANES), F32), jnp.zeros((n_ctx, LANES), F32)
    head_id = jnp.arange(ATTN_WIDTH) // HEAD_DIM
    bd = (head_id[:, None] == head_id[None, :]).astype(BF16)
    bd_kv = bd[:KV_WIDTH, :KV_WIDTH]

    f32_weights = (w_in, w_out, w_ff1, w_ff3, w_ff2)
    w_in_b, w_out_b, w1_b, w3_b, w2_b = (w[0].astype(BF16) for w in f32_weights)
    gq = jnp.tile(q_norm_g, (1, N_Q_HEADS)).reshape(DEPTH, 1, ATTN_WIDTH)
    gk = jnp.tile(k_norm_g, (1, N_KV_HEADS)).reshape(DEPTH, 1, KV_WIDTH)
    dww8 = jnp.broadcast_to(dw_w.reshape(DEPTH, CONV_KERNEL, 1, CONV_WIDTH),
                            (DEPTH, CONV_KERNEL, SUBLANES, CONV_WIDTH))
    row = lambda p, l: p[l].reshape(1, -1)

    for l in range(DEPTH):
        last = l == DEPTH - 1
        q_l, kk_l, vv_l, a_l = _inproj_call(x, mods[l], w_in_b, gq[l], gk[l], cos_l, sin_l, bd,
                                            tm=lat_tm, mod_row=None)
        if last:
            kk_c, vv_c = _kvproj_call(ctx, mods[l], w_in_b[:, ATTN_WIDTH:ATTN_WIDTH + 2 * KV_WIDTH],
                                      gk[l], cos_c, sin_c, bd_kv, tm=ctx_tm, mod_row=ctx_row)
        else:
            q_c, kk_c, vv_c, a_c = _inproj_call(ctx, mods[l], w_in_b, gq[l], gk[l], cos_c, sin_c, bd,
                                                tm=ctx_tm, mod_row=ctx_row)
        tail_args = (dww8[l], row(dw_b, l), row(conv_ln_g, l), row(conv_ln_b, l), w_out_b,
                     row(ln1_g, l), row(ln1_b, l), w1_b, w3_b, w2_b, row(ln2_g, l), row(ln2_b, l))

        if last:
            o_l, _ = _attn_call(q_l, [kk_c, vv_c, kk_l, vv_l], tq=ATTN_TILE_ROWS)
        else:
            o_l, (w_in_b, w_out_b, w1_b, w3_b, w2_b) = _attn_call(
                q_l, [kk_c, vv_c, kk_l, vv_l], tq=ATTN_TILE_ROWS, cast_layer=l + 1, cast_weights=f32_weights)
        x = _mixffn_call(a_l, o_l, x, mods[l], *tail_args, tm=lat_tm, mod_row=None)

        if not last:
            o_c, _ = _attn_call(q_c, [kk_c, vv_c], tq=ctx_tm)
            ctx = _mixffn_call(a_c, o_c, ctx, mods[l], *tail_args, tm=ctx_tm, mod_row=ctx_row)
    return x
```

```python
import functools

import jax
import jax.numpy as jnp
from jax import lax
from jax.experimental import pallas as pl
from jax.experimental.pallas import tpu as pltpu

F32 = jnp.float32
BF16 = jnp.bfloat16

D_MODEL = 1024
DEPTH = 4
GRID_W = 64
HEAD_DIM = 64
N_Q_HEADS = 8
N_KV_HEADS = 2
ATTN_WIDTH = N_Q_HEADS * HEAD_DIM
KV_WIDTH = N_KV_HEADS * HEAD_DIM
CONV_WIDTH = D_MODEL - ATTN_WIDTH
CONV_KERNEL = 31
CONV_HALF = CONV_KERNEL // 2
D_FF = 2816
N_MOD = 6
ROPE_THETA = 10000.0
DN_ALPHA = (2 * DEPTH) ** 0.25
LN_EPS = 1e-6
RMS_EPS = 1e-6
ATTN_SCALE = HEAD_DIM ** -0.5
IN_WIDTH = ATTN_WIDTH + 2 * KV_WIDTH + 2 * CONV_WIDTH

LANES = 128
SUBLANES = 8
HALO = 16
ATTN_UNIT_ROWS = 512
ATTN_TILE_ROWS = 1024
ROW_TILE = 512
CONV_TILES = 4
CHAIN_DELAY_ROLLS = 4
MOD_ROWS = 16
VMEM_LIMIT = 52 * 1024 * 1024

NT_DIMS = (((1,), (1,)), ((), ()))


def _const_spec(shape):
    zeros = (0,) * len(shape)
    return pl.BlockSpec(shape, lambda *_: zeros, pipeline_mode=pl.Buffered(1))


def _layer_norm(z, g, b, eps=LN_EPS):
    mu = jnp.mean(z, axis=-1, keepdims=True)
    zc = z - mu
    var = jnp.mean(zc * zc, axis=-1, keepdims=True)
    return zc * lax.rsqrt(var + eps) * g + b


def _deepnorm(x, gate, y, g, b):
    return _layer_norm(x + (gate * (1.0 / DN_ALPHA)) * y, g, b, eps=LN_EPS / DN_ALPHA ** 2)


def _params(n_axes=2):
    return pltpu.CompilerParams(dimension_semantics=("arbitrary",) * n_axes, vmem_limit_bytes=VMEM_LIMIT)


def _ada_kernel(c_ref, w_ref, b_ref, o_ref):
    c = c_ref[...]
    s = (c * jax.nn.sigmoid(c)).astype(BF16)
    o_ref[...] = jnp.dot(s, w_ref[...].astype(BF16), preferred_element_type=F32) + b_ref[...]


def _ada_call(c_all, w_ada, b_ada):
    return pl.pallas_call(
        _ada_kernel,
        grid=(DEPTH, N_MOD),
        in_specs=[
            pl.BlockSpec((MOD_ROWS, D_MODEL), lambda l, j: (0, 0)),
            pl.BlockSpec((None, D_MODEL, D_MODEL), lambda l, j: (l, 0, j)),
            pl.BlockSpec((None, 1, D_MODEL), lambda l, j: (l, 0, j)),
        ],
        out_specs=pl.BlockSpec((None, MOD_ROWS, D_MODEL), lambda l, j: (l, 0, j)),
        out_shape=jax.ShapeDtypeStruct((DEPTH, MOD_ROWS, N_MOD * D_MODEL), F32),
        compiler_params=_params(),
        name="ada_mod",
    )(c_all, w_ada, b_ada)


def _head_norm_rope(z, gain, bd, cos, sin):
    ssq = jnp.dot((z * z).astype(BF16), bd, preferred_element_type=F32)
    zn = z * lax.rsqrt(ssq * (1.0 / HEAD_DIM) + RMS_EPS) * gain
    outs = []
    for c0 in range(0, z.shape[1], LANES):
        zc = zn[:, c0:c0 + LANES]
        up = pltpu.roll(zc, LANES - 16, 1)
        dn = pltpu.roll(zc, 16, 1)
        lane = lax.broadcasted_iota(jnp.int32, zc.shape, 1)
        partner = jnp.where((lane & 16) == 0, up, dn)
        outs.append(zc * cos + partner * sin)
    return outs


def _kv_variants(z, fill):
    zs = pltpu.roll(z, HEAD_DIM, 1)
    lo = lax.broadcasted_iota(jnp.int32, z.shape, 1) < HEAD_DIM
    spare = jnp.full_like(z, fill)
    return jnp.concatenate(
        [jnp.where(lo, z, spare), jnp.where(lo, spare, zs), jnp.where(lo, zs, spare), jnp.where(lo, spare, z)],
        axis=-1).astype(BF16)


def _cast_specs(cast_weights, cast_layer, n_seg, nt):
    in_specs, out_specs, shapes = [], [], []
    for w in cast_weights:
        _, n_rows, n_cols = w.shape
        slab = n_rows // (n_seg * nt)
        assert slab * n_seg * nt == n_rows and slab % (2 * SUBLANES) == 0
        in_specs.append(pl.BlockSpec((None, slab, n_cols), lambda b, t: (cast_layer, b * nt + t, 0)))
        out_specs.append(pl.BlockSpec((slab, n_cols), lambda b, t: (b * nt + t, 0)))
        shapes.append(jax.ShapeDtypeStruct((n_rows, n_cols), BF16))
    return in_specs, out_specs, shapes


def _inproj_kernel(x_ref, mod_ref, w_ref, gq_ref, gk_ref, cos_ref, sin_ref, bd_ref, *refs, n_cast):
    q_ref, kk_ref, vv_ref, a_ref = refs[n_cast:n_cast + 4]
    for w_in_ref, w_out_ref in zip(refs[:n_cast], refs[n_cast + 4:]):
        w_out_ref[...] = w_in_ref[...].astype(BF16)
    h = (x_ref[...] * (1.0 + mod_ref[1:2, :]) + mod_ref[0:1, :]).astype(BF16)
    proj = jnp.dot(h, w_ref[...], preferred_element_type=F32)
    cos, sin = cos_ref[...], sin_ref[...]
    q = proj[:, :ATTN_WIDTH]
    k = proj[:, ATTN_WIDTH:ATTN_WIDTH + KV_WIDTH]
    v = proj[:, ATTN_WIDTH + KV_WIDTH:ATTN_WIDTH + 2 * KV_WIDTH]
    u = proj[:, ATTN_WIDTH + 2 * KV_WIDTH:ATTN_WIDTH + 2 * KV_WIDTH + CONV_WIDTH]
    g = proj[:, ATTN_WIDTH + 2 * KV_WIDTH + CONV_WIDTH:]
    qs = _head_norm_rope(q, gq_ref[...], bd_ref[...], cos, sin)
    q_ref[...] = (jnp.concatenate(qs, axis=-1) * ATTN_SCALE).astype(BF16)
    (kr,) = _head_norm_rope(k, gk_ref[...], bd_ref[:KV_WIDTH, :KV_WIDTH], cos, sin)
    kk_ref[...] = _kv_variants(kr, 0.0)
    vv_ref[...] = _kv_variants(v, 1.0)
    a_ref[...] = (u * jax.nn.sigmoid(g)).astype(BF16)


def _kvproj_kernel(x_ref, mod_ref, w_ref, gk_ref, cos_ref, sin_ref, bd_ref, kk_ref, vv_ref):
    h = (x_ref[...] * (1.0 + mod_ref[1:2, :]) + mod_ref[0:1, :]).astype(BF16)
    proj = jnp.dot(h, w_ref[...], preferred_element_type=F32)
    (kr,) = _head_norm_rope(proj[:, :KV_WIDTH], gk_ref[...], bd_ref[...], cos_ref[...], sin_ref[...])
    kk_ref[...] = _kv_variants(kr, 0.0)
    vv_ref[...] = _kv_variants(proj[:, KV_WIDTH:], 1.0)


def _seg_spec(tm, width):
    return pl.BlockSpec((None, tm, width), lambda b, t: (b, t, 0))


def _mod_spec(mod_row):
    if mod_row is None:
        return pl.BlockSpec((None, N_MOD, D_MODEL), lambda b, t: (b, 0, 0))
    return pl.BlockSpec((None, N_MOD, D_MODEL), lambda b, t: (mod_row, 0, 0))


def _inproj_call(x, mods, w_in, gq, gk, cos, sin, bd, *, tm, mod_row, cast_layer=None, cast_weights=()):
    n_seg, seg_len, _ = x.shape
    nt = seg_len // tm
    tab_spec = pl.BlockSpec((tm, LANES), lambda b, t: (t, 0))
    cast_in_specs, cast_out_specs, cast_shapes = _cast_specs(cast_weights, cast_layer, n_seg, nt)
    outs = pl.pallas_call(
        functools.partial(_inproj_kernel, n_cast=len(cast_weights)),
        grid=(n_seg, nt),
        in_specs=[_seg_spec(tm, D_MODEL), _mod_spec(mod_row), _const_spec((D_MODEL, IN_WIDTH)),
                  _const_spec((1, ATTN_WIDTH)), _const_spec((1, KV_WIDTH)), tab_spec, tab_spec,
                  _const_spec((ATTN_WIDTH, ATTN_WIDTH))] + cast_in_specs,
        out_specs=[_seg_spec(tm, ATTN_WIDTH), _seg_spec(tm, 4 * LANES), _seg_spec(tm, 4 * LANES),
                   _seg_spec(tm, CONV_WIDTH)] + cast_out_specs,
        out_shape=[jax.ShapeDtypeStruct((n_seg, seg_len, ATTN_WIDTH), BF16),
                   jax.ShapeDtypeStruct((n_seg, seg_len, 4 * LANES), BF16),
                   jax.ShapeDtypeStruct((n_seg, seg_len, 4 * LANES), BF16),
                   jax.ShapeDtypeStruct((n_seg, seg_len, CONV_WIDTH), BF16)] + cast_shapes,
        compiler_params=_params(),
        name="in_proj",
    )(x, mods, w_in, gq, gk, cos, sin, bd, *cast_weights)
    return outs[:4], outs[4:]


def _kvproj_call(x, mods, w_kv, gk, cos, sin, bd_kv, *, tm, mod_row):
    n_seg, seg_len, _ = x.shape
    nt = seg_len // tm
    tab_spec = pl.BlockSpec((tm, LANES), lambda b, t: (t, 0))
    return pl.pallas_call(
        _kvproj_kernel,
        grid=(n_seg, nt),
        in_specs=[_seg_spec(tm, D_MODEL), _mod_spec(mod_row), _const_spec((D_MODEL, 2 * KV_WIDTH)),
                  _const_spec((1, KV_WIDTH)), tab_spec, tab_spec, _const_spec((KV_WIDTH, KV_WIDTH))],
        out_specs=[_seg_spec(tm, 4 * LANES), _seg_spec(tm, 4 * LANES)],
        out_shape=[jax.ShapeDtypeStruct((n_seg, seg_len, 4 * LANES), BF16),
                   jax.ShapeDtypeStruct((n_seg, seg_len, 4 * LANES), BF16)],
        compiler_params=_params(),
        name="kv_proj",
    )(x, mods, w_kv, gk, cos, sin, bd_kv)


def _attn_kernel(q_ref, *refs, n_src, n_cast):
    sources = [(refs[2 * i], refs[2 * i + 1]) for i in range(n_src)]
    cast_in = refs[2 * n_src:2 * n_src + n_cast]
    o_ref = refs[2 * n_src + n_cast]
    cast_out = refs[2 * n_src + n_cast + 1:]
    for w_ref, wb_ref in zip(cast_in, cast_out):
        wb_ref[...] = w_ref[...].astype(BF16)
    rows = min(q_ref.shape[0], ATTN_UNIT_ROWS)
    for j in range(ATTN_WIDTH // LANES):
        kvh = j // 2
        for r0 in range(0, q_ref.shape[0], rows):
            qp = q_ref[r0:r0 + rows, j * LANES:(j + 1) * LANES]
            halves = []
            for par in range(2):
                c0 = (2 * kvh + par) * LANES
                scores = [lax.dot_general(qp, k_ref[:, c0:c0 + LANES], NT_DIMS, preferred_element_type=F32)
                          for k_ref, _ in sources]
                m = scores[0].max(axis=-1, keepdims=True)
                for s in scores[1:]:
                    m = jnp.maximum(m, s.max(axis=-1, keepdims=True))
                o = None
                for (_, v_ref), s in zip(sources, scores):
                    part = jnp.dot(jnp.exp(s - m).astype(BF16), v_ref[:, c0:c0 + LANES],
                                   preferred_element_type=F32)
                    o = part if o is None else o + part
                halves.append(o * (1.0 / pltpu.roll(o, HEAD_DIM, 1)))
            lo = lax.broadcasted_iota(jnp.int32, halves[0].shape, 1) < HEAD_DIM
            o_ref[r0:r0 + rows, j * LANES:(j + 1) * LANES] = jnp.where(lo, halves[0], halves[1]).astype(BF16)


def _attn_call(q, sources, *, tq, cast_layer=None, cast_weights=()):
    n_seg, seg_len, _ = q.shape
    nt = seg_len // tq
    kv_specs = [pl.BlockSpec((None, s.shape[1], 4 * LANES), lambda b, t: (b, 0, 0)) for s in sources]
    cast_in_specs, cast_out_specs, cast_shapes = _cast_specs(cast_weights, cast_layer, n_seg, nt)
    outs = pl.pallas_call(
        functools.partial(_attn_kernel, n_src=len(sources) // 2, n_cast=len(cast_weights)),
        grid=(n_seg, nt),
        in_specs=[_seg_spec(tq, ATTN_WIDTH)] + kv_specs + cast_in_specs,
        out_specs=[_seg_spec(tq, ATTN_WIDTH)] + cast_out_specs,
        out_shape=[jax.ShapeDtypeStruct((n_seg, seg_len, ATTN_WIDTH), BF16)] + cast_shapes,
        compiler_params=_params(),
        name="attention",
    )(q, *sources, *cast_weights)
    return outs[0], outs[1:]


def _conv_stage(a_ref, t, pad_ref, *, tm, nt):
    seg_len = tm * nt
    zero_halo = jnp.zeros((HALO, CONV_WIDTH), F32)
    if nt == 1:
        start = 0
        top, bot = zero_halo, zero_halo
    else:
        start = pl.multiple_of(t * tm, tm)
        top_start = pl.multiple_of(jnp.maximum(start - HALO, 0), HALO)
        bot_start = pl.multiple_of(jnp.minimum(start + tm, seg_len - HALO), HALO)
        top = jnp.where(t == 0, zero_halo, a_ref[pl.ds(top_start, HALO), :].astype(F32))
        bot = jnp.where(t == nt - 1, zero_halo, a_ref[pl.ds(bot_start, HALO), :].astype(F32))
    halo_tiles = HALO // SUBLANES
    row_tiles = tm // SUBLANES
    pad_ref[0:halo_tiles] = top.reshape(halo_tiles, SUBLANES, CONV_WIDTH)
    pad_ref[halo_tiles:halo_tiles + row_tiles] = (
        a_ref[pl.ds(start, tm), :].astype(F32).reshape(row_tiles, SUBLANES, CONV_WIDTH))
    pad_ref[halo_tiles + row_tiles:] = bot.reshape(halo_tiles, SUBLANES, CONV_WIDTH)


def _ordered_after(x, dep):
    assert CHAIN_DELAY_ROLLS >= 2
    bits = lax.bitcast_convert_type(dep, jnp.uint32)
    for _ in range(CHAIN_DELAY_ROLLS):
        bits = pltpu.roll(bits, 1, 1) >> 16
    return x + lax.bitcast_convert_type(bits, F32)


def _conv_column(c0, dww_ref, dwb_ref, pad_ref, shift_ref, y_ref):
    n_tiles = pad_ref.shape[0]
    row_tiles = y_ref.shape[0]
    sub = lax.broadcasted_iota(jnp.int32, (n_tiles - 1, SUBLANES, LANES), 1)
    for s in range(1, SUBLANES):
        rot = pltpu.roll(pad_ref[:, :, c0:c0 + LANES], SUBLANES - s, 1)
        shift_ref[s - 1] = jnp.where(sub < SUBLANES - s, rot[:-1], rot[1:])
    bias = jnp.broadcast_to(dwb_ref[:, c0:c0 + LANES], (SUBLANES, LANES))
    acc = None
    for k0 in range(0, row_tiles, CONV_TILES):
        init = bias if acc is None else _ordered_after(bias, acc[0])
        acc = jnp.broadcast_to(init, (CONV_TILES, SUBLANES, LANES))
        for j in range(CONV_KERNEL):
            m, s = divmod(HALO - CONV_HALF + j, SUBLANES)
            if s == 0:
                src = pad_ref[k0 + m:k0 + m + CONV_TILES, :, c0:c0 + LANES]
            else:
                src = shift_ref[s - 1, k0 + m:k0 + m + CONV_TILES]
            acc = acc + src * dww_ref[j, :, c0:c0 + LANES]
        y_ref[k0:k0 + CONV_TILES, :, c0:c0 + LANES] = acc


def _mixffn_kernel(a_ref, o_ref, x_ref, mod_ref, dww_ref, dwb_ref, cg_ref, cb_ref, wout_ref,
                   lg1_ref, lb1_ref, w1_ref, w3_ref, w2_ref, lg2_ref, lb2_ref, xo_ref,
                   conv_ref, pad_ref, shift_ref, y_ref, *, tm, nt):
    i = pl.program_id(0)
    n_steps = pl.num_programs(0)
    t = jnp.minimum(i, n_steps - 2) % nt

    @pl.when(i == 0)
    def _():
        conv_ref[1] = jnp.zeros((tm, CONV_WIDTH), BF16)

    mixed = jnp.concatenate([o_ref[...], conv_ref[(i + 1) % 2]], axis=-1)
    y = jnp.dot(mixed, wout_ref[...], preferred_element_type=F32)
    xm = _deepnorm(x_ref[...], mod_ref[2:3, :], y, lg1_ref[...], lb1_ref[...])
    h = (xm * (1.0 + mod_ref[4:5, :]) + mod_ref[3:4, :]).astype(BF16)
    a1 = jnp.dot(h, w1_ref[...], preferred_element_type=F32)
    a3 = jnp.dot(h, w3_ref[...], preferred_element_type=F32)
    hid = (a1 * jax.nn.sigmoid(a1) * a3).astype(BF16)
    yf = jnp.dot(hid, w2_ref[...], preferred_element_type=F32)
    xo_ref[...] = _deepnorm(xm, mod_ref[5:6, :], yf, lg2_ref[...], lb2_ref[...])

    _conv_stage(a_ref, t, pad_ref, tm=tm, nt=nt)
    for c0 in range(0, CONV_WIDTH, LANES):
        _conv_column(c0, dww_ref, dwb_ref, pad_ref, shift_ref, y_ref)
    yn = _layer_norm(y_ref[...].reshape(tm, CONV_WIDTH), cg_ref[...], cb_ref[...])
    conv_ref[i % 2] = (yn * jax.nn.sigmoid(yn)).astype(BF16)


def _mixffn_call(a, o, x, mods, dww8, dwb, cg, cb, w_out, lg1, lb1, w1, w3, w2, lg2, lb2, *, tm, mod_row):
    n_seg, seg_len, _ = x.shape
    nt = seg_len // tm
    n_total = n_seg * nt
    prev = lambda i: jnp.maximum(i - 1, 0)
    row_spec = lambda width: pl.BlockSpec((tm, width), lambda i: (prev(i), 0))
    if mod_row is None:
        mod_spec = pl.BlockSpec((None, N_MOD, D_MODEL), lambda i: (prev(i) // nt, 0, 0))
    else:
        mod_spec = pl.BlockSpec((None, N_MOD, D_MODEL), lambda i: (mod_row, 0, 0))
    n_win = (tm + 2 * HALO) // SUBLANES
    out = pl.pallas_call(
        functools.partial(_mixffn_kernel, tm=tm, nt=nt),
        grid=(n_total + 1,),
        in_specs=[pl.BlockSpec((None, seg_len, CONV_WIDTH), lambda i: (jnp.minimum(i, n_total - 1) // nt, 0, 0)),
                  row_spec(ATTN_WIDTH), row_spec(D_MODEL), mod_spec,
                  _const_spec((CONV_KERNEL, SUBLANES, CONV_WIDTH)), _const_spec((1, CONV_WIDTH)),
                  _const_spec((1, CONV_WIDTH)), _const_spec((1, CONV_WIDTH)),
                  _const_spec((D_MODEL, D_MODEL)), _const_spec((1, D_MODEL)), _const_spec((1, D_MODEL)),
                  _const_spec((D_MODEL, D_FF)), _const_spec((D_MODEL, D_FF)), _const_spec((D_FF, D_MODEL)),
                  _const_spec((1, D_MODEL)), _const_spec((1, D_MODEL))],
        out_specs=row_spec(D_MODEL),
        out_shape=jax.ShapeDtypeStruct((n_seg * seg_len, D_MODEL), F32),
        scratch_shapes=[pltpu.VMEM((2, tm, CONV_WIDTH), BF16),
                        pltpu.VMEM((n_win, SUBLANES, CONV_WIDTH), F32),
                        pltpu.VMEM((SUBLANES - 1, n_win - 1, SUBLANES, LANES), F32),
                        pltpu.VMEM((tm // SUBLANES, SUBLANES, CONV_WIDTH), F32)],
        compiler_params=_params(1),
        name="mix_ffn",
    )(a, o.reshape(n_seg * seg_len, ATTN_WIDTH), x.reshape(n_seg * seg_len, D_MODEL), mods, dww8, dwb, cg, cb,
      w_out, lg1, lb1, w1, w3, w2, lg2, lb2)
    return out.reshape(n_seg, seg_len, D_MODEL)


def _rope_tables(n):
    rows = n // GRID_W
    r, col = jnp.meshgrid(jnp.arange(rows), jnp.arange(GRID_W), indexing="ij")
    pos = jnp.stack([r.reshape(-1), col.reshape(-1)], axis=-1).astype(F32)
    n_freq = HEAD_DIM // 4
    freqs = ROPE_THETA ** (-jnp.arange(n_freq, dtype=F32) / n_freq)
    ang = pos[:, :, None] * freqs
    cos, sin = jnp.cos(ang), jnp.sin(ang)
    cos_head = jnp.concatenate([cos[:, 0], cos[:, 0], cos[:, 1], cos[:, 1]], axis=-1)
    sin_head = jnp.concatenate([-sin[:, 0], sin[:, 0], -sin[:, 1], sin[:, 1]], axis=-1)
    reps = LANES // HEAD_DIM
    return jnp.tile(cos_head, (1, reps)), jnp.tile(sin_head, (1, reps))


def kernel(x, c, ctx, c_ctx, w_ada, b_ada, w_in, q_norm_g, k_norm_g, dw_w, dw_b, conv_ln_g, conv_ln_b,
           w_out, ln1_g, ln1_b, w_ff1, w_ff3, w_ff2, ln2_g, ln2_b):
    n_batch, n_lat, _ = x.shape
    n_ctx = ctx.shape[1]
    assert n_batch + 1 <= MOD_ROWS
    assert n_lat % ATTN_TILE_ROWS == 0 and n_lat % ROW_TILE == 0 and n_ctx % HALO == 0
    ctx_row = n_batch
    lat_tm, ctx_tm = ROW_TILE, n_ctx

    c_all = jnp.concatenate(
        [c, c_ctx[None, :], jnp.zeros((MOD_ROWS - n_batch - 1, D_MODEL), F32)], axis=0)
    mods = _ada_call(c_all, w_ada, b_ada.reshape(DEPTH, 1, N_MOD * D_MODEL))
    mods = mods.reshape(DEPTH, MOD_ROWS, N_MOD, D_MODEL)

    cos_l, sin_l = _rope_tables(n_lat)
    cos_c, sin_c = jnp.ones((n_ctx, LANES), F32), jnp.zeros((n_ctx, LANES), F32)
    head_id = jnp.arange(ATTN_WIDTH) // HEAD_DIM
    bd = (head_id[:, None] == head_id[None, :]).astype(BF16)
    bd_kv = bd[:KV_WIDTH, :KV_WIDTH]

    f32_weights = (w_in, w_out, w_ff1, w_ff3, w_ff2)
    w_in_b, w2_b = w_in[0].astype(BF16), w_ff2[0].astype(BF16)
    w_out_b = w1_b = w3_b = None
    gq = jnp.tile(q_norm_g, (1, N_Q_HEADS)).reshape(DEPTH, 1, ATTN_WIDTH)
    gk = jnp.tile(k_norm_g, (1, N_KV_HEADS)).reshape(DEPTH, 1, KV_WIDTH)
    dww8 = jnp.broadcast_to(dw_w.reshape(DEPTH, CONV_KERNEL, 1, CONV_WIDTH),
                            (DEPTH, CONV_KERNEL, SUBLANES, CONV_WIDTH))
    row = lambda p, l: p[l].reshape(1, -1)

    for l in range(DEPTH):
        last = l == DEPTH - 1
        if l == 0:
            (q_l, kk_l, vv_l, a_l), (w_out_b, w1_b, w3_b) = _inproj_call(
                x, mods[l], w_in_b, gq[l], gk[l], cos_l, sin_l, bd, tm=lat_tm, mod_row=None,
                cast_layer=0, cast_weights=(w_out, w_ff1, w_ff3))
        else:
            (q_l, kk_l, vv_l, a_l), _ = _inproj_call(x, mods[l], w_in_b, gq[l], gk[l], cos_l, sin_l, bd,
                                                     tm=lat_tm, mod_row=None)
        if last:
            kk_c, vv_c = _kvproj_call(ctx, mods[l], w_in_b[:, ATTN_WIDTH:ATTN_WIDTH + 2 * KV_WIDTH],
                                      gk[l], cos_c, sin_c, bd_kv, tm=ctx_tm, mod_row=ctx_row)
        else:
            (q_c, kk_c, vv_c, a_c), _ = _inproj_call(ctx, mods[l], w_in_b, gq[l], gk[l], cos_c, sin_c, bd,
                                                     tm=ctx_tm, mod_row=ctx_row)
        tail_args = (dww8[l], row(dw_b, l), row(conv_ln_g, l), row(conv_ln_b, l), w_out_b,
                     row(ln1_g, l), row(ln1_b, l), w1_b, w3_b, w2_b, row(ln2_g, l), row(ln2_b, l))

        if last:
            o_l, _ = _attn_call(q_l, [kk_c, vv_c, kk_l, vv_l], tq=ATTN_TILE_ROWS)
        else:
            o_l, (w_in_b, w_out_b, w1_b, w3_b, w2_b) = _attn_call(
                q_l, [kk_c, vv_c, kk_l, vv_l], tq=ATTN_TILE_ROWS, cast_layer=l + 1, cast_weights=f32_weights)
        x = _mixffn_call(a_l, o_l, x, mods[l], *tail_args, tm=lat_tm, mod_row=None)

        if not last:
            o_c, _ = _attn_call(q_c, [kk_c, vv_c], tq=ctx_tm)
            ctx = _mixffn_call(a_c, o_c, ctx, mods[l], *tail_args, tm=ctx_tm, mod_row=ctx_row)
    return x
```
